```python
import math
import jax, jax.numpy as jnp
from jax import lax
import numpy as np

D_MODEL = 2048
BATCH = 8
SEQ = 4096
DEPTH = 1
DEC_BATCH = 8
DEC_SEQ = 16
PAST_LEN = 1024

CHUNK = 64
SSM_WIDTH = D_MODEL // 2
SSM_GROUP = 16
SSM_GROUPS = SSM_WIDTH // SSM_GROUP
SSM_STATE = 64
HEAD_DIM = 64
N_HEADS = (D_MODEL // 2) // HEAD_DIM
N_KV_HEADS = max(1, N_HEADS // 8)
GQA_GROUP = N_HEADS // N_KV_HEADS
ATTN_WIDTH = N_HEADS * HEAD_DIM
KV_WIDTH = N_KV_HEADS * HEAD_DIM
WINDOW = 128
BAND_CHUNKS = -(-WINDOW // CHUNK)
SCALE = HEAD_DIM ** -0.5
N_BUCKETS = 32
MAX_DISTANCE = 128
EPS = 1e-6
IN_WIDTH = 2 * SSM_WIDTH + 2 * ATTN_WIDTH + 2 * KV_WIDTH + 2 * D_MODEL
SPLITS = (SSM_WIDTH,
          2 * SSM_WIDTH,
          2 * SSM_WIDTH + ATTN_WIDTH,
          2 * SSM_WIDTH + ATTN_WIDTH + KV_WIDTH,
          2 * SSM_WIDTH + ATTN_WIDTH + 2 * KV_WIDTH,
          2 * SSM_WIDTH + 2 * ATTN_WIDTH + 2 * KV_WIDTH,
          2 * SSM_WIDTH + 2 * ATTN_WIDTH + 2 * KV_WIDTH + D_MODEL)

kernel_name = "hybrid_s5_swa_sink_stream_step"


def _rmsnorm(x, g):
    xf = x.astype(jnp.float32)
    y = xf * lax.rsqrt(jnp.mean(xf * xf, axis=-1, keepdims=True) + EPS) * g.astype(jnp.float32)
    return y.astype(x.dtype)


def _t5_bucket(rel):
    half = N_BUCKETS // 2
    n = -rel
    ret = jnp.where(n < 0, half, 0)
    n = jnp.abs(n)
    max_exact = half // 2
    nf = jnp.maximum(n, 1).astype(jnp.float32)
    large = max_exact + (jnp.log(nf / max_exact) / math.log(MAX_DISTANCE / max_exact)
                         * (half - max_exact)).astype(jnp.int32)
    large = jnp.minimum(large, half - 1)
    return ret + jnp.where(n < max_exact, n, large)


def _rel_bias(rel, table):
    b = table.astype(jnp.float32)[_t5_bucket(rel)]
    b = jnp.moveaxis(b, -1, 0)
    return b.reshape(N_KV_HEADS, GQA_GROUP, rel.shape[0], rel.shape[1])


def _sink_softmax(s, sinks):
    sk = sinks.astype(jnp.float32).reshape(N_KV_HEADS, GQA_GROUP, 1, 1)
    m = jnp.maximum(jnp.max(s, axis=-1, keepdims=True), sk)
    e = jnp.exp(s - m)
    return e / (jnp.sum(e, axis=-1, keepdims=True) + jnp.exp(sk - m))


def _mixer_inputs(x, norm_gain, w_in, q_gain, k_gain):
    B, L = x.shape[0], x.shape[1]
    h = _rmsnorm(x, norm_gain) @ w_in
    u, z_a, q, k, v, z_b, g_a, g_b = jnp.split(h, SPLITS, axis=-1)
    q = _rmsnorm(q.reshape(B, L, N_HEADS, HEAD_DIM), q_gain)
    k = _rmsnorm(k.reshape(B, L, N_KV_HEADS, HEAD_DIM), k_gain)
    v = v.reshape(B, L, N_KV_HEADS, HEAD_DIM)
    return u, z_a, q, k, v, z_b, g_a, g_b


def _discretize(a_re, a_im, log_dt, b_re, b_im):
    a_re = a_re.astype(jnp.float32)
    a_im = a_im.astype(jnp.float32)
    b_re = b_re.astype(jnp.float32)
    b_im = b_im.astype(jnp.float32)
    dt = jnp.exp(log_dt.astype(jnp.float32))[:, None]
    mag = jnp.exp(a_re * dt)
    ang = a_im * dt
    lam_re = mag * jnp.cos(ang)
    lam_im = mag * jnp.sin(ang)
    den = a_re * a_re + a_im * a_im
    cr = ((lam_re - 1.0) * a_re + lam_im * a_im) / den
    ci = (lam_im * a_re - (lam_re - 1.0) * a_im) / den
    bb_re = cr[..., None] * b_re - ci[..., None] * b_im
    bb_im = cr[..., None] * b_im + ci[..., None] * b_re
    return lam_re, lam_im, bb_re, bb_im


def _combine(e1, e2):
    a1r, a1i, b1r, b1i = e1
    a2r, a2i, b2r, b2i = e2
    return (a2r * a1r - a2i * a1i,
            a2r * a1i + a2i * a1r,
            a2r * b1r - a2i * b1i + b2r,
            a2r * b1i + a2i * b1r + b2i)


def _s5(u, h0_re, h0_im, lam_re, lam_im, bb_re, bb_im, c_re, c_im, d_skip):
    B, L = u.shape[0], u.shape[1]
    uf = u.astype(jnp.float32).reshape(B, L, SSM_GROUPS, SSM_GROUP)
    bu_re = jnp.einsum('blgc,gpc->blgp', uf, bb_re)
    bu_im = jnp.einsum('blgc,gpc->blgp', uf, bb_im)
    a_re = jnp.broadcast_to(lam_re, (1, L, SSM_GROUPS, SSM_STATE))
    a_im = jnp.broadcast_to(lam_im, (1, L, SSM_GROUPS, SSM_STATE))
    acum_re, acum_im, h_re, h_im = lax.associative_scan(_combine, (a_re, a_im, bu_re, bu_im), axis=1)
    if h0_re is not None:
        h0r = h0_re.astype(jnp.float32)[:, None]
        h0i = h0_im.astype(jnp.float32)[:, None]
        h_re, h_im = (h_re + acum_re * h0r - acum_im * h0i,
                      h_im + acum_re * h0i + acum_im * h0r)
    y = (jnp.einsum('blgp,gcp->blgc', h_re, c_re.astype(jnp.float32))
         - jnp.einsum('blgp,gcp->blgc', h_im, c_im.astype(jnp.float32)))
    y = y.reshape(B, L, SSM_WIDTH) + d_skip.astype(jnp.float32) * u.astype(jnp.float32)
    return y.astype(u.dtype), h_re[:, -1], h_im[:, -1]


def _band_attention(q, k, v, rel_table, sinks):
    B, L = q.shape[0], q.shape[1]
    nc = L // CHUNK
    span = (BAND_CHUNKS + 1) * CHUNK
    qb = q.reshape(B, nc, CHUNK, N_KV_HEADS, GQA_GROUP, HEAD_DIM)

    def band(t):
        tc = t.reshape(B, nc, CHUNK, N_KV_HEADS, HEAD_DIM)
        tp = jnp.concatenate([jnp.zeros((B, BAND_CHUNKS) + tc.shape[2:], t.dtype), tc], axis=1)
        return jnp.concatenate([tp[:, j:j + nc] for j in range(BAND_CHUNKS + 1)], axis=2)

    kb, vb = band(k), band(v)
    s = jnp.einsum('bnqhgd,bnshd->bnhgqs', qb, kb).astype(jnp.float32) * SCALE
    q_off = jnp.arange(CHUNK)
    s_off = jnp.arange(span)
    rel = s_off[None, :] - BAND_CHUNKS * CHUNK - q_off[:, None]
    s = s + _rel_bias(rel, rel_table)
    key_chunk = jnp.arange(nc)[:, None] - BAND_CHUNKS + s_off[None, :] // CHUNK
    s = jnp.where((key_chunk >= 0)[None, :, None, None, None, :], s, -jnp.inf)
    p = _sink_softmax(s, sinks)
    o = jnp.einsum('bnhgqs,bnshd->bnqhgd', p.astype(vb.dtype), vb)
    return o.reshape(B, L, ATTN_WIDTH)


def _cached_attention(q, k, v, cache_k, cache_v, rel_table, sinks):
    Bd, T = q.shape[0], q.shape[1]
    R = cache_k.shape[1]
    kk = jnp.concatenate([cache_k.astype(k.dtype), k], axis=1)
    vv = jnp.concatenate([cache_v.astype(v.dtype), v], axis=1)
    qg = q.reshape(Bd, T, N_KV_HEADS, GQA_GROUP, HEAD_DIM)
    s = jnp.einsum('bqhgd,bshd->bhgqs', qg, kk).astype(jnp.float32) * SCALE
    rel = jnp.arange(R + T)[None, :] - R - jnp.arange(T)[:, None]
    s = s + _rel_bias(rel, rel_table)
    p = _sink_softmax(s, sinks)
    o = jnp.einsum('bhgqs,bshd->bqhgd', p.astype(vv.dtype), vv)
    return o.reshape(Bd, T, ATTN_WIDTH)


def _merge(x, y_ssm, z_a, o_attn, z_b, g_a, g_b, w_glu, b_glu, w_out_a, w_out_b, w_o):
    g = jax.nn.gelu(y_ssm, approximate=False)
    br_a = g * jax.nn.sigmoid(g @ w_glu + b_glu) * jax.nn.silu(z_a)
    br_b = o_attn * jax.nn.silu(z_b)
    mixed = jax.nn.sigmoid(g_a) * (br_a @ w_out_a) + jax.nn.sigmoid(g_b) * (br_b @ w_out_b)
    return x + mixed @ w_o


def setup_inputs(seed: int = 0) -> dict:
    key = jax.random.key(seed)
    ks = jax.random.split(key, 26)
    f32 = jnp.float32
    n = lambda k, s: jax.random.normal(k, s, f32)
    rows = min(WINDOW, PAST_LEN)
    return {
        "x_prompt": n(ks[0], (BATCH, SEQ, D_MODEL)),
        "x_sample": n(ks[1], (DEC_BATCH, DEC_SEQ, D_MODEL)),
        "cache_k": n(ks[2], (DEPTH, DEC_BATCH, rows, N_KV_HEADS, HEAD_DIM)),
        "cache_v": n(ks[3], (DEPTH, DEC_BATCH, rows, N_KV_HEADS, HEAD_DIM)),
        "state_ssm_re": 0.5 * n(ks[4], (DEPTH, DEC_BATCH, SSM_GROUPS, SSM_STATE)),
        "state_ssm_im": 0.5 * n(ks[5], (DEPTH, DEC_BATCH, SSM_GROUPS, SSM_STATE)),
        "norm_gain": 1.0 + 0.01 * n(ks[6], (DEPTH, D_MODEL)),
        "w_in": n(ks[7], (DEPTH, D_MODEL, IN_WIDTH)) * D_MODEL ** -0.5,
        "ssm_a_re": -0.5 + 0.01 * n(ks[8], (DEPTH, SSM_GROUPS, SSM_STATE)),
        "ssm_a_im": math.pi * jnp.arange(SSM_STATE, dtype=f32) + 0.01 * n(ks[9], (DEPTH, SSM_GROUPS, SSM_STATE)),
        "ssm_log_dt": jax.random.uniform(ks[10], (DEPTH, SSM_GROUPS), f32, math.log(1e-3), math.log(1e-1)),
        "ssm_b_re": n(ks[11], (DEPTH, SSM_GROUPS, SSM_STATE, SSM_GROUP)) * (2 * SSM_GROUP) ** -0.5,
        "ssm_b_im": n(ks[12], (DEPTH, SSM_GROUPS, SSM_STATE, SSM_GROUP)) * (2 * SSM_GROUP) ** -0.5,
        "ssm_c_re": n(ks[13], (DEPTH, SSM_GROUPS, SSM_GROUP, SSM_STATE)) * SSM_STATE ** -0.5,
        "ssm_c_im": n(ks[14], (DEPTH, SSM_GROUPS, SSM_GROUP, SSM_STATE)) * SSM_STATE ** -0.5,
        "ssm_d": n(ks[15], (DEPTH, SSM_WIDTH)),
        "w_glu": n(ks[16], (DEPTH, SSM_WIDTH, SSM_WIDTH)) * SSM_WIDTH ** -0.5,
        "b_glu": 0.01 * n(ks[17], (DEPTH, SSM_WIDTH)),
        "q_gain": 1.0 + 0.01 * n(ks[18], (DEPTH, HEAD_DIM)),
        "k_gain": 1.0 + 0.01 * n(ks[19], (DEPTH, HEAD_DIM)),
        "attn_sinks": n(ks[20], (DEPTH, N_HEADS)),
        "rel_bias": 0.5 * n(ks[21], (N_BUCKETS, N_HEADS)),
        "w_out_a": n(ks[22], (DEPTH, SSM_WIDTH, D_MODEL)) * SSM_WIDTH ** -0.5,
        "w_out_b": n(ks[23], (DEPTH, ATTN_WIDTH, D_MODEL)) * ATTN_WIDTH ** -0.5,
        "w_o": n(ks[24], (DEPTH, D_MODEL, D_MODEL)) * D_MODEL ** -0.5,
    }


def reference(x_prompt, x_sample, cache_k, cache_v, state_ssm_re, state_ssm_im,
              norm_gain, w_in, ssm_a_re, ssm_a_im, ssm_log_dt, ssm_b_re, ssm_b_im,
              ssm_c_re, ssm_c_im, ssm_d, w_glu, b_glu, q_gain, k_gain, attn_sinks,
              rel_bias, w_out_a, w_out_b, w_o):
    y_p, y_s = x_prompt, x_sample
    p_re, p_im, p_k, p_v = [], [], [], []
    s_re, s_im, s_k, s_v = [], [], [], []
    for l in range(DEPTH):
        lam_re, lam_im, bb_re, bb_im = _discretize(ssm_a_re[l], ssm_a_im[l], ssm_log_dt[l],
                                                   ssm_b_re[l], ssm_b_im[l])
        u, z_a, q, k, v, z_b, g_a, g_b = _mixer_inputs(y_p, norm_gain[l], w_in[l], q_gain[l], k_gain[l])
        y_ssm, h_re, h_im = _s5(u, None, None, lam_re, lam_im, bb_re, bb_im,
                                ssm_c_re[l], ssm_c_im[l], ssm_d[l])
        o = _band_attention(q, k, v, rel_bias, attn_sinks[l])
        y_p = _merge(y_p, y_ssm, z_a, o, z_b, g_a, g_b, w_glu[l], b_glu[l], w_out_a[l], w_out_b[l], w_o[l])
        keep = min(WINDOW, k.shape[1])
        p_re.append(h_re)
        p_im.append(h_im)
        p_k.append(k[:, -keep:])
        p_v.append(v[:, -keep:])
        u, z_a, q, k, v, z_b, g_a, g_b = _mixer_inputs(y_s, norm_gain[l], w_in[l], q_gain[l], k_gain[l])
        y_ssm, h_re, h_im = _s5(u, state_ssm_re[l], state_ssm_im[l], lam_re, lam_im, bb_re, bb_im,
                                ssm_c_re[l], ssm_c_im[l], ssm_d[l])
        o = _cached_attention(q, k, v, cache_k[l], cache_v[l], rel_bias, attn_sinks[l])
        y_s = _merge(y_s, y_ssm, z_a, o, z_b, g_a, g_b, w_glu[l], b_glu[l], w_out_a[l], w_out_b[l], w_o[l])
        s_re.append(h_re)
        s_im.append(h_im)
        s_k.append(k)
        s_v.append(v)
    prompt_ssm_re = jnp.stack(p_re)
    prompt_ssm_im = jnp.stack(p_im)
    prompt_k = jnp.stack(p_k)
    prompt_v = jnp.stack(p_v)
    sample_ssm_re = jnp.stack(s_re)
    sample_ssm_im = jnp.stack(s_im)
    sample_k = jnp.stack(s_k)
    sample_v = jnp.stack(s_v)
    return (y_p, y_s, prompt_ssm_re, prompt_ssm_im, prompt_k, prompt_v,
            sample_ssm_re, sample_ssm_im, sample_k, sample_v)
```

```python
import functools
import math

import jax
import jax.numpy as jnp
from jax import lax
from jax.experimental import pallas as pl
from jax.experimental.pallas import tpu as pltpu

F32 = jnp.float32
BF16 = jnp.bfloat16

LANES = 128
SUBLANES = 8
VMEM_LIMIT = 56 * 1024 * 1024

CHUNK = 64
HEAD_DIM = 64
GQA_GROUP = 8
SSM_GROUP = 16
SSM_STATE = 64
WINDOW = 128
BAND_CHUNKS = 2
N_BUCKETS = 32
MAX_DISTANCE = 128
EPS = 1e-6
SCALE = HEAD_DIM ** -0.5
SLAB_GROUPS = LANES // SSM_GROUP
SLAB_STATES = SLAB_GROUPS * SSM_STATE
SCAN_LANES = 1024


def _inproj_body(x_ref, g_ref, w_ref, o_ref, xn_ref):
    @pl.when(pl.program_id(1) == 0)
    def _():
        x = x_ref[...]
        ms = jnp.mean(x * x, axis=-1, keepdims=True)
        xn_ref[...] = (x * lax.rsqrt(ms + EPS) * g_ref[...]).astype(BF16)

    o_ref[...] = jnp.dot(xn_ref[...], w_ref[...], preferred_element_type=F32)


def _inproj(x2, gain, w_bf, tm, tn):
    n, d = x2.shape
    width = w_bf.shape[1]
    return pl.pallas_call(
        _inproj_body,
        grid=(n // tm, width // tn),
        in_specs=[pl.BlockSpec((tm, d), lambda i, j: (i, 0)),
                  pl.BlockSpec((1, d), lambda i, j: (0, 0)),
                  pl.BlockSpec((d, tn), lambda i, j: (0, j))],
        out_specs=pl.BlockSpec((tm, tn), lambda i, j: (i, j)),
        out_shape=jax.ShapeDtypeStruct((n, width), F32),
        scratch_shapes=[pltpu.VMEM((tm, d), BF16)],
        compiler_params=pltpu.CompilerParams(
            dimension_semantics=("parallel", "arbitrary"), vmem_limit_bytes=VMEM_LIMIT),
        name="inproj",
    )(x2, gain, w_bf)


def _s5_body(u_ref, h0_ref, lam_ref, bmat_ref, cmat_ref, d_ref, y_ref, hout_ref,
             utb_ref, ytb_ref, bu_ref, hst_ref, *, nb, tl, n_slab, n_state):
    m = tl * nb

    @pl.when(pl.program_id(0) == 0)
    def _():
        hst_ref[...] = h0_ref[...]

    for s in range(n_slab):
        for b in range(nb):
            utb_ref[s, pl.ds(b, tl, stride=nb), :] = u_ref[b, :, s * LANES:(s + 1) * LANES]

    for s in range(n_slab):
        bu = jnp.dot(utb_ref[s].astype(BF16), bmat_ref[s], preferred_element_type=F32)
        bu_ref[:, s * SLAB_STATES:(s + 1) * SLAB_STATES] = bu[:, :SLAB_STATES]
        bu_ref[:, n_state + s * SLAB_STATES:n_state + (s + 1) * SLAB_STATES] = bu[:, SLAB_STATES:]

    for c in range(n_state // SCAN_LANES):
        re = slice(c * SCAN_LANES, (c + 1) * SCAN_LANES)
        im = slice(n_state + c * SCAN_LANES, n_state + (c + 1) * SCAN_LANES)
        lr = lam_ref[0, :, re]
        li = lam_ref[1, :, re]

        def step(t, carry, re=re, im=im, lr=lr, li=li):
            hr, hi = carry
            r0 = pl.multiple_of(t * nb, SUBLANES)
            nhr = lr * hr - li * hi + bu_ref[pl.ds(r0, nb), re]
            nhi = lr * hi + li * hr + bu_ref[pl.ds(r0, nb), im]
            bu_ref[pl.ds(r0, nb), re] = nhr
            bu_ref[pl.ds(r0, nb), im] = nhi
            return nhr, nhi

        hr, hi = lax.fori_loop(0, tl, step, (hst_ref[:, re], hst_ref[:, im]), unroll=2)
        hst_ref[:, re] = hr
        hst_ref[:, im] = hi

    hout_ref[...] = hst_ref[...]

    for s in range(n_slab):
        hre = bu_ref[:, s * SLAB_STATES:(s + 1) * SLAB_STATES].astype(BF16)
        him = bu_ref[:, n_state + s * SLAB_STATES:n_state + (s + 1) * SLAB_STATES].astype(BF16)
        y = (jnp.dot(hre, cmat_ref[s, :SLAB_STATES, :], preferred_element_type=F32)
             + jnp.dot(him, cmat_ref[s, SLAB_STATES:, :], preferred_element_type=F32))
        ytb_ref[s] = y + d_ref[s] * utb_ref[s]

    for s in range(n_slab):
        for b in range(nb):
            y_ref[b, :, s * LANES:(s + 1) * LANES] = ytb_ref[s, pl.ds(b, tl, stride=nb), :]


def _s5(h3, h0, lam_b, bmat, cmat, d_slab, tl):
    nb, seq, _ = h3.shape
    n_slab = bmat.shape[0]
    width = n_slab * LANES
    n_state = n_slab * SLAB_STATES
    m = tl * nb
    body = functools.partial(_s5_body, nb=nb, tl=tl, n_slab=n_slab, n_state=n_state)
    const = lambda *shape: pl.BlockSpec(shape, lambda i: (0,) * len(shape))
    return pl.pallas_call(
        body,
        grid=(seq // tl,),
        in_specs=[pl.BlockSpec((nb, tl, width), lambda i: (0, i, 0)),
                  const(nb, 2 * n_state),
                  const(2, nb, n_state),
                  const(n_slab, LANES, 2 * SLAB_STATES),
                  const(n_slab, 2 * SLAB_STATES, LANES),
                  const(n_slab, 1, LANES)],
        out_specs=[pl.BlockSpec((nb, tl, width), lambda i: (0, i, 0)),
                   const(nb, 2 * n_state)],
        out_shape=[jax.ShapeDtypeStruct((nb, seq, width), F32),
                   jax.ShapeDtypeStruct((nb, 2 * n_state), F32)],
        scratch_shapes=[pltpu.VMEM((n_slab, m, LANES), F32),
                        pltpu.VMEM((n_slab, m, LANES), F32),
                        pltpu.VMEM((m, 2 * n_state), F32),
                        pltpu.VMEM((nb, 2 * n_state), F32)],
        compiler_params=pltpu.CompilerParams(
            dimension_semantics=("arbitrary",), vmem_limit_bytes=VMEM_LIMIT),
        name="s5",
    )(h3, h0, lam_b, bmat, cmat, d_slab)


def _head_norm(t, gain2):
    lo = lax.broadcasted_iota(jnp.int32, (1, LANES), 1) < HEAD_DIM
    sq = t * t
    ss_lo = jnp.sum(jnp.where(lo, sq, 0.0), axis=-1, keepdims=True)
    ss_hi = jnp.sum(jnp.where(lo, 0.0, sq), axis=-1, keepdims=True)
    r = jnp.where(lo, lax.rsqrt(ss_lo / HEAD_DIM + EPS), lax.rsqrt(ss_hi / HEAD_DIM + EPS))
    return t * r * gain2


def _attend(q, qgain2, ka, kb, va, vb, bias_ref, sink_ref, kv, maskadd):
    t = q.shape[0]
    s = ka.shape[0]
    n_pair = GQA_GROUP // 2
    lo = lax.broadcasted_iota(jnp.int32, (1, LANES), 1) < HEAD_DIM
    zero = jnp.zeros((), BF16)
    top_k, bot_k = (ka, kb) if kv == 0 else (kb, ka)
    top_v, bot_v = (va, vb) if kv == 0 else (vb, va)
    kbd = jnp.concatenate([jnp.where(lo, top_k, zero), jnp.where(lo, zero, bot_k)], axis=0)
    vbd = jnp.concatenate([jnp.where(lo, top_v, zero), jnp.where(lo, zero, bot_v)], axis=0)
    qn = jnp.concatenate(
        [_head_norm(q[:, i * LANES:(i + 1) * LANES], qgain2) for i in range(n_pair)], axis=0).astype(BF16)
    sc = lax.dot_general(qn, kbd, (((1,), (1,)), ((), ())), preferred_element_type=F32)
    sc = sc * SCALE + bias_ref[kv]
    if maskadd is not None:
        sc = sc + maskadd
    first = lax.broadcasted_iota(jnp.int32, (1, 2 * s), 1) < s
    sk0 = sink_ref[kv, :, 0:1]
    sk1 = sink_ref[kv, :, 1:2]
    m0 = jnp.maximum(jnp.max(jnp.where(first, sc, -jnp.inf), axis=-1, keepdims=True), sk0)
    m1 = jnp.maximum(jnp.max(jnp.where(first, -jnp.inf, sc), axis=-1, keepdims=True), sk1)
    e = jnp.exp(sc - jnp.where(first, m0, m1))
    d0 = jnp.sum(jnp.where(first, e, 0.0), axis=-1, keepdims=True) + jnp.exp(sk0 - m0)
    d1 = jnp.sum(jnp.where(first, 0.0, e), axis=-1, keepdims=True) + jnp.exp(sk1 - m1)
    o = jnp.dot(e.astype(BF16), vbd, preferred_element_type=F32)
    o = o * jnp.where(lo, 1.0 / d0, 1.0 / d1)
    return jnp.concatenate([o[i * t:(i + 1) * t] for i in range(n_pair)], axis=1)


def _band_attn_body(q_ref, k_ref, v_ref, qg_ref, kg_ref, bias_ref, sink_ref, o_ref, pk_ref, pv_ref,
                    ka_ref, kb_ref, va_ref, vb_ref, *, seq, qb):
    j = pl.program_id(1)
    pad = BAND_CHUNKS * CHUNK
    span = pad + CHUNK
    rows = min(512, seq)

    @pl.when(j == 0)
    def _():
        zeros = jnp.zeros((pad, LANES), BF16)
        for r in (ka_ref, kb_ref, va_ref, vb_ref):
            r[0:pad, :] = zeros

        def fill(i, carry):
            r0 = pl.multiple_of(i * rows, rows)
            kn = _head_norm(k_ref[0, pl.ds(r0, rows), :], kg_ref[...])
            vv = v_ref[0, pl.ds(r0, rows), :]
            ka_ref[pl.ds(pad + r0, rows), :] = kn.astype(BF16)
            kb_ref[pl.ds(pad + r0, rows), :] = pltpu.roll(kn, HEAD_DIM, 1).astype(BF16)
            va_ref[pl.ds(pad + r0, rows), :] = vv.astype(BF16)
            vb_ref[pl.ds(pad + r0, rows), :] = pltpu.roll(vv, HEAD_DIM, 1).astype(BF16)
            return carry

        lax.fori_loop(0, seq // rows, fill, 0)
        pk_ref[0] = _head_norm(k_ref[0, seq - WINDOW:seq, :], kg_ref[...])
        pv_ref[0] = v_ref[0, seq - WINDOW:seq, :]

    col = lax.broadcasted_iota(jnp.int32, (1, 2 * span), 1)
    col = jnp.where(col < span, col, col - span)

    def chunk(c, carry):
        cg = j * (qb // CHUNK) + c
        q0 = pl.multiple_of(c * CHUNK, CHUNK)
        k0 = pl.multiple_of(cg * CHUNK, CHUNK)
        maskadd = jnp.where(col >= (BAND_CHUNKS - cg) * CHUNK, 0.0, -jnp.inf)
        ka = ka_ref[pl.ds(k0, span), :]
        kb = kb_ref[pl.ds(k0, span), :]
        va = va_ref[pl.ds(k0, span), :]
        vb = vb_ref[pl.ds(k0, span), :]
        for kv in range(2):
            w = GQA_GROUP * HEAD_DIM
            q = q_ref[0, pl.ds(q0, CHUNK), kv * w:(kv + 1) * w]
            o_ref[0, pl.ds(q0, CHUNK), kv * w:(kv + 1) * w] = _attend(
                q, qg_ref[...], ka, kb, va, vb, bias_ref, sink_ref, kv, maskadd)
        return carry

    lax.fori_loop(0, qb // CHUNK, chunk, 0)


def _band_attn(h3, qcol, kcol, vcol, qgain2, kgain2, bias, sinkcol, qb):
    nb, seq, _ = h3.shape
    aw = 2 * GQA_GROUP * HEAD_DIM
    pad = BAND_CHUNKS * CHUNK
    body = functools.partial(_band_attn_body, seq=seq, qb=qb)
    const = lambda *shape: pl.BlockSpec(shape, lambda b, j: (0,) * len(shape))
    return pl.pallas_call(
        body,
        grid=(nb, seq // qb),
        in_specs=[pl.BlockSpec((1, qb, aw), lambda b, j: (b, j, qcol // aw)),
                  pl.BlockSpec((1, seq, LANES), lambda b, j: (b, 0, kcol // LANES)),
                  pl.BlockSpec((1, seq, LANES), lambda b, j: (b, 0, vcol // LANES)),
                  const(1, LANES), const(1, LANES),
                  const(*bias.shape), const(*sinkcol.shape)],
        out_specs=[pl.BlockSpec((1, qb, aw), lambda b, j: (b, j, 0)),
                   pl.BlockSpec((1, WINDOW, LANES), lambda b, j: (b, 0, 0)),
                   pl.BlockSpec((1, WINDOW, LANES), lambda b, j: (b, 0, 0))],
        out_shape=[jax.ShapeDtypeStruct((nb, seq, aw), F32),
                   jax.ShapeDtypeStruct((nb, WINDOW, LANES), F32),
                   jax.ShapeDtypeStruct((nb, WINDOW, LANES), F32)],
        scratch_shapes=[pltpu.VMEM((seq + pad, LANES), BF16) for _ in range(4)],
        compiler_params=pltpu.CompilerParams(
            dimension_semantics=("parallel", "arbitrary"), vmem_limit_bytes=VMEM_LIMIT),
        name="band_attn",
    )(h3, h3, h3, qgain2, kgain2, bias, sinkcol)


def _cached_attn_body(q_ref, k_ref, v_ref, ck_ref, cv_ref, qg_ref, kg_ref, bias_ref, sink_ref,
                      o_ref, sk_ref, sv_ref):
    kn = _head_norm(k_ref[0], kg_ref[...])
    vn = v_ref[0]
    sk_ref[0] = kn
    sv_ref[0] = vn
    kk = jnp.concatenate([ck_ref[0], kn], axis=0)
    vv = jnp.concatenate([cv_ref[0], vn], axis=0)
    ka = kk.astype(BF16)
    kb = pltpu.roll(kk, HEAD_DIM, 1).astype(BF16)
    va = vv.astype(BF16)
    vb = pltpu.roll(vv, HEAD_DIM, 1).astype(BF16)
    w = GQA_GROUP * HEAD_DIM
    for kv in range(2):
        o_ref[0, :, kv * w:(kv + 1) * w] = _attend(
            q_ref[0, :, kv * w:(kv + 1) * w], qg_ref[...], ka, kb, va, vb, bias_ref, sink_ref, kv, None)


def _cached_attn(h3, qcol, kcol, vcol, cache_k, cache_v, qgain2, kgain2, bias, sinkcol):
    nb, t, _ = h3.shape
    aw = 2 * GQA_GROUP * HEAD_DIM
    r = cache_k.shape[1]
    const = lambda *shape: pl.BlockSpec(shape, lambda b: (0,) * len(shape))
    return pl.pallas_call(
        _cached_attn_body,
        grid=(nb,),
        in_specs=[pl.BlockSpec((1, t, aw), lambda b: (b, 0, qcol // aw)),
                  pl.BlockSpec((1, t, LANES), lambda b: (b, 0, kcol // LANES)),
                  pl.BlockSpec((1, t, LANES), lambda b: (b, 0, vcol // LANES)),
                  pl.BlockSpec((1, r, LANES), lambda b: (b, 0, 0)),
                  pl.BlockSpec((1, r, LANES), lambda b: (b, 0, 0)),
                  const(1, LANES), const(1, LANES),
                  const(*bias.shape), const(*sinkcol.shape)],
        out_specs=[pl.BlockSpec((1, t, aw), lambda b: (b, 0, 0)),
                   pl.BlockSpec((1, t, LANES), lambda b: (b, 0, 0)),
                   pl.BlockSpec((1, t, LANES), lambda b: (b, 0, 0))],
        out_shape=[jax.ShapeDtypeStruct((nb, t, aw), F32),
                   jax.ShapeDtypeStruct((nb, t, LANES), F32),
                   jax.ShapeDtypeStruct((nb, t, LANES), F32)],
        compiler_params=pltpu.CompilerParams(
            dimension_semantics=("parallel",), vmem_limit_bytes=VMEM_LIMIT),
        name="cached_attn",
    )(h3, h3, h3, cache_k, cache_v, qgain2, kgain2, bias, sinkcol)


def _sigmoid(x):
    return 1.0 / (1.0 + jnp.exp(-x))


def _merge_body(x_ref, y_ref, za_ref, o_ref, zb_ref, ga_ref, gb_ref,
                wglu_ref, bglu_ref, woa_ref, wob_ref, wo_ref, out_ref):
    y = y_ref[...]
    g = 0.5 * y * (1.0 + lax.erf(y * (2.0 ** -0.5)))
    gl = jnp.dot(g.astype(BF16), wglu_ref[...], preferred_element_type=F32) + bglu_ref[...]
    za = za_ref[...]
    br_a = g * _sigmoid(gl) * (za * _sigmoid(za))
    zb = zb_ref[...]
    br_b = o_ref[...] * (zb * _sigmoid(zb))
    mixed = (_sigmoid(ga_ref[...]) * jnp.dot(br_a.astype(BF16), woa_ref[...], preferred_element_type=F32)
             + _sigmoid(gb_ref[...]) * jnp.dot(br_b.astype(BF16), wob_ref[...], preferred_element_type=F32))
    out_ref[...] = x_ref[...] + jnp.dot(mixed.astype(BF16), wo_ref[...], preferred_element_type=F32)


def _merge(x2, y2, o2, h2, cols, wglu, bglu, woa, wob, wo, tm):
    n, d = x2.shape
    sw = y2.shape[1]
    aw = o2.shape[1]
    za_col, zb_col, ga_col, gb_col = cols
    row = lambda width, col: pl.BlockSpec((tm, width), lambda i: (i, col // width))
    weight = lambda a: pl.BlockSpec(a.shape, lambda i: (0, 0), pipeline_mode=pl.Buffered(1))
    return pl.pallas_call(
        _merge_body,
        grid=(n // tm,),
        in_specs=[row(d, 0), row(sw, 0), row(sw, za_col), row(aw, 0), row(aw, zb_col),
                  row(d, ga_col), row(d, gb_col),
                  weight(wglu), weight(bglu), weight(woa), weight(wob), weight(wo)],
        out_specs=row(d, 0),
        out_shape=jax.ShapeDtypeStruct((n, d), F32),
        compiler_params=pltpu.CompilerParams(
            dimension_semantics=("parallel",), vmem_limit_bytes=VMEM_LIMIT),
        name="merge",
    )(x2, y2, h2, o2, h2, h2, h2, wglu, bglu, woa, wob, wo)


def _discretize(a_re, a_im, log_dt, b_re, b_im):
    dt = jnp.exp(log_dt)[:, None]
    mag = jnp.exp(a_re * dt)
    ang = a_im * dt
    lam_re = mag * jnp.cos(ang)
    lam_im = mag * jnp.sin(ang)
    den = a_re * a_re + a_im * a_im
    cr = ((lam_re - 1.0) * a_re + lam_im * a_im) / den
    ci = (lam_im * a_re - (lam_re - 1.0) * a_im) / den
    bb_re = cr[..., None] * b_re - ci[..., None] * b_im
    bb_im = cr[..., None] * b_im + ci[..., None] * b_re
    return lam_re, lam_im, bb_re, bb_im


def _slab_blockdiag(t):
    n_slab, g, a, b = t.shape
    eye = jnp.eye(g, dtype=t.dtype)
    return jnp.einsum("sgab,gh->sgahb", t, eye).reshape(n_slab, g * a, g * b)


def _t5_bucket(rel):
    half = N_BUCKETS // 2
    n = -rel
    ret = jnp.where(n < 0, half, 0)
    n = jnp.abs(n)
    max_exact = half // 2
    nf = jnp.maximum(n, 1).astype(F32)
    large = max_exact + (jnp.log(nf / max_exact) / math.log(MAX_DISTANCE / max_exact)
                         * (half - max_exact)).astype(jnp.int32)
    large = jnp.minimum(large, half - 1)
    return ret + jnp.where(n < max_exact, n, large)


def _pair_bias(rel, table):
    t, s = rel.shape
    b = table.astype(F32)[_t5_bucket(rel)]
    b = b.reshape(t, s, 2, GQA_GROUP // 2, 2)
    return jnp.transpose(b, (2, 3, 0, 4, 1)).reshape(2, (GQA_GROUP // 2) * t, 2 * s)


def _pair_sinks(sinks, t):
    sk = sinks.astype(F32).reshape(2, GQA_GROUP // 2, 1, 2)
    return jnp.broadcast_to(sk, (2, GQA_GROUP // 2, t, 2)).reshape(2, (GQA_GROUP // 2) * t, 2)


def kernel(x_prompt, x_sample, cache_k, cache_v, state_ssm_re, state_ssm_im, norm_gain, w_in, ssm_a_re, ssm_a_im, ssm_log_dt, ssm_b_re, ssm_b_im, ssm_c_re, ssm_c_im, ssm_d, w_glu, b_glu, q_gain, k_gain, attn_sinks, rel_bias, w_out_a, w_out_b, w_o):
    nb, seq, d = x_prompt.shape
    db, dseq, _ = x_sample.shape
    n_groups, n_st = ssm_a_re.shape[1:]
    sw = n_groups * SSM_GROUP
    aw = 2 * GQA_GROUP * HEAD_DIM
    kvw = 2 * HEAD_DIM
    n_slab = sw // LANES
    n_state = n_groups * n_st
    l = 0

    s_u, s_za, s_q, s_k, s_v, s_zb, s_ga = sw, 2 * sw, 2 * sw + aw, 2 * sw + aw + kvw, 2 * sw + aw + 2 * kvw, \
        2 * sw + 2 * aw + 2 * kvw, 2 * sw + 2 * aw + 2 * kvw + d
    w = w_in[l]
    w_bf = jnp.concatenate([w[:, :s_u], w[:, s_u:s_za], w[:, s_za:s_q], w[:, s_v:s_zb], w[:, s_zb:s_ga],
                            w[:, s_ga:], w[:, s_q:s_k], w[:, s_k:s_v]], axis=1).astype(BF16)
    c_u, c_za, c_q, c_zb, c_ga, c_gb = 0, sw, 2 * sw, 2 * sw + aw, 2 * sw + 2 * aw, 2 * sw + 2 * aw + d
    c_k = c_gb + d
    c_v = c_k + kvw
    gain = norm_gain[l].astype(F32).reshape(1, d)

    lam_re, lam_im, bb_re, bb_im = _discretize(
        ssm_a_re[l].astype(F32), ssm_a_im[l].astype(F32), ssm_log_dt[l].astype(F32),
        ssm_b_re[l].astype(F32), ssm_b_im[l].astype(F32))
    lam_b = jnp.broadcast_to(jnp.stack([lam_re.reshape(-1), lam_im.reshape(-1)])[:, None, :], (2, nb, n_state))
    slab = lambda t: t.reshape(n_slab, SLAB_GROUPS, *t.shape[1:])
    bmat = jnp.concatenate([_slab_blockdiag(jnp.swapaxes(slab(bb_re), 2, 3)),
                            _slab_blockdiag(jnp.swapaxes(slab(bb_im), 2, 3))], axis=2).astype(BF16)
    cmat = jnp.concatenate([_slab_blockdiag(jnp.swapaxes(slab(ssm_c_re[l].astype(F32)), 2, 3)),
                            _slab_blockdiag(jnp.swapaxes(slab(-ssm_c_im[l].astype(F32)), 2, 3))], axis=1).astype(BF16)
    d_slab = ssm_d[l].astype(F32).reshape(n_slab, 1, LANES)

    qgain2 = jnp.tile(q_gain[l].astype(F32), 2).reshape(1, LANES)
    kgain2 = jnp.tile(k_gain[l].astype(F32), 2).reshape(1, LANES)
    span = (BAND_CHUNKS + 1) * CHUNK
    rel_p = jnp.arange(span)[None, :] - BAND_CHUNKS * CHUNK - jnp.arange(CHUNK)[:, None]
    bias_p = _pair_bias(rel_p, rel_bias)
    sink_p = _pair_sinks(attn_sinks[l], CHUNK)
    rows = cache_k.shape[2]
    rel_s = jnp.arange(rows + dseq)[None, :] - rows - jnp.arange(dseq)[:, None]
    bias_s = _pair_bias(rel_s, rel_bias)
    sink_s = _pair_sinks(attn_sinks[l], dseq)

    wglu = w_glu[l].astype(BF16)
    bglu = b_glu[l].astype(F32).reshape(1, sw)
    woa = w_out_a[l].astype(BF16)
    wob = w_out_b[l].astype(BF16)
    wo = w_o[l].astype(BF16)
    merge_cols = (c_za, c_zb, c_ga, c_gb)

    xp2 = x_prompt.reshape(nb * seq, d)
    hp = _inproj(xp2, gain, w_bf, tm=min(1024, nb * seq), tn=768)
    hp3 = hp.reshape(nb, seq, -1)
    yp, hfin_p = _s5(hp3, jnp.zeros((nb, 2 * n_state), F32), lam_b, bmat, cmat, d_slab, tl=min(64, seq))
    op, pk, pv = _band_attn(hp3, c_q, c_k, c_v, qgain2, kgain2, bias_p, sink_p, qb=min(512, seq))
    y_p = _merge(xp2, yp.reshape(nb * seq, sw), op.reshape(nb * seq, aw), hp, merge_cols,
                 wglu, bglu, woa, wob, wo, tm=256).reshape(nb, seq, d)

    xs2 = x_sample.reshape(db * dseq, d)
    hs = _inproj(xs2, gain, w_bf, tm=db * dseq, tn=768)
    hs3 = hs.reshape(db, dseq, -1)
    h0 = jnp.concatenate([state_ssm_re[l].reshape(db, n_state), state_ssm_im[l].reshape(db, n_state)],
                         axis=1).astype(F32)
    ys, hfin_s = _s5(hs3, h0, lam_b, bmat, cmat, d_slab, tl=dseq)
    ck = cache_k[l].reshape(db, rows, kvw).astype(F32)
    cv = cache_v[l].reshape(db, rows, kvw).astype(F32)
    os_, sk, sv = _cached_attn(hs3, c_q, c_k, c_v, ck, cv, qgain2, kgain2, bias_s, sink_s)
    y_s = _merge(xs2, ys.reshape(db * dseq, sw), os_.reshape(db * dseq, aw), hs, merge_cols,
                 wglu, bglu, woa, wob, wo, tm=db * dseq).reshape(db, dseq, d)

    st = lambda h, nbb: (h[:, :n_state].reshape(1, nbb, n_groups, n_st), h[:, n_state:].reshape(1, nbb, n_groups, n_st))
    p_re, p_im = st(hfin_p, nb)
    s_re, s_im = st(hfin_s, db)
    kvshape = lambda a: a.reshape(1, a.shape[0], a.shape[1], 2, HEAD_DIM)
    return (y_p, y_s, p_re, p_im, kvshape(pk), kvshape(pv), s_re, s_im, kvshape(sk), kvshape(sv))
```

```python
import functools
import math

import jax
import jax.numpy as jnp
from jax import lax
from jax.experimental import pallas as pl
from jax.experimental.pallas import tpu as pltpu

F32 = jnp.float32
BF16 = jnp.bfloat16

LANES = 128
SUBLANES = 8
VMEM_LIMIT = 56 * 1024 * 1024

CHUNK = 64
HEAD_DIM = 64
GQA_GROUP = 8
SSM_GROUP = 16
SSM_STATE = 64
WINDOW = 128
BAND_CHUNKS = 2
N_BUCKETS = 32
MAX_DISTANCE = 128
EPS = 1e-6
SCALE = HEAD_DIM ** -0.5
LOG2E = math.log2(math.e)
SLAB_GROUPS = LANES // SSM_GROUP
SLAB_STATES = SLAB_GROUPS * SSM_STATE
SCAN_LANES = 1024


def _inproj_body(x_ref, g_ref, w_ref, o_ref, xn_ref):
    @pl.when(pl.program_id(1) == 0)
    def _():
        x = x_ref[...]
        ms = jnp.mean(x * x, axis=-1, keepdims=True)
        xn_ref[...] = (x * lax.rsqrt(ms + EPS) * g_ref[...]).astype(BF16)

    o_ref[...] = jnp.dot(xn_ref[...], w_ref[...], preferred_element_type=F32)


def _inproj(x2, gain, w_bf, tm, tn):
    n, d = x2.shape
    width = w_bf.shape[1]
    return pl.pallas_call(
        _inproj_body,
        grid=(n // tm, width // tn),
        in_specs=[pl.BlockSpec((tm, d), lambda i, j: (i, 0)),
                  pl.BlockSpec((1, d), lambda i, j: (0, 0)),
                  pl.BlockSpec((d, tn), lambda i, j: (0, j))],
        out_specs=pl.BlockSpec((tm, tn), lambda i, j: (i, j)),
        out_shape=jax.ShapeDtypeStruct((n, width), F32),
        scratch_shapes=[pltpu.VMEM((tm, d), BF16)],
        compiler_params=pltpu.CompilerParams(
            dimension_semantics=("parallel", "arbitrary"), vmem_limit_bytes=VMEM_LIMIT),
        name="inproj",
    )(x2, gain, w_bf)


def _s5_body(u_ref, h0_ref, lam_ref, bmat_ref, cmat_ref, d_ref, y_ref, hout_ref,
             utb_ref, ytb_ref, bu_ref, hst_ref, *, nb, tl, n_slab, n_state):
    m = tl * nb

    @pl.when(pl.program_id(0) == 0)
    def _():
        hst_ref[...] = h0_ref[...]

    for s in range(n_slab):
        for b in range(nb):
            utb_ref[s, pl.ds(b, tl, stride=nb), :] = u_ref[b, :, s * LANES:(s + 1) * LANES]

    for s in range(n_slab):
        bu = jnp.dot(utb_ref[s].astype(BF16), bmat_ref[s], preferred_element_type=F32)
        bu_ref[:, s * SLAB_STATES:(s + 1) * SLAB_STATES] = bu[:, :SLAB_STATES]
        bu_ref[:, n_state + s * SLAB_STATES:n_state + (s + 1) * SLAB_STATES] = bu[:, SLAB_STATES:]

    for c in range(n_state // SCAN_LANES):
        re = slice(c * SCAN_LANES, (c + 1) * SCAN_LANES)
        im = slice(n_state + c * SCAN_LANES, n_state + (c + 1) * SCAN_LANES)
        lr = lam_ref[0, :, re]
        li = lam_ref[1, :, re]

        def step(t, carry, re=re, im=im, lr=lr, li=li):
            hr, hi = carry
            r0 = pl.multiple_of(t * nb, SUBLANES)
            nhr = lr * hr - li * hi + bu_ref[pl.ds(r0, nb), re]
            nhi = lr * hi + li * hr + bu_ref[pl.ds(r0, nb), im]
            bu_ref[pl.ds(r0, nb), re] = nhr
            bu_ref[pl.ds(r0, nb), im] = nhi
            return nhr, nhi

        hr, hi = lax.fori_loop(0, tl, step, (hst_ref[:, re], hst_ref[:, im]), unroll=2)
        hst_ref[:, re] = hr
        hst_ref[:, im] = hi

    hout_ref[...] = hst_ref[...]

    for s in range(n_slab):
        hre = bu_ref[:, s * SLAB_STATES:(s + 1) * SLAB_STATES].astype(BF16)
        him = bu_ref[:, n_state + s * SLAB_STATES:n_state + (s + 1) * SLAB_STATES].astype(BF16)
        y = (jnp.dot(hre, cmat_ref[s, :SLAB_STATES, :], preferred_element_type=F32)
             + jnp.dot(him, cmat_ref[s, SLAB_STATES:, :], preferred_element_type=F32))
        ytb_ref[s] = y + d_ref[s] * utb_ref[s]

    for s in range(n_slab):
        for b in range(nb):
            y_ref[b, :, s * LANES:(s + 1) * LANES] = ytb_ref[s, pl.ds(b, tl, stride=nb), :]


def _s5(h3, h0, lam_b, bmat, cmat, d_slab, tl):
    nb, seq, _ = h3.shape
    n_slab = bmat.shape[0]
    width = n_slab * LANES
    n_state = n_slab * SLAB_STATES
    m = tl * nb
    body = functools.partial(_s5_body, nb=nb, tl=tl, n_slab=n_slab, n_state=n_state)
    const = lambda *shape: pl.BlockSpec(shape, lambda i: (0,) * len(shape))
    return pl.pallas_call(
        body,
        grid=(seq // tl,),
        in_specs=[pl.BlockSpec((nb, tl, width), lambda i: (0, i, 0)),
                  const(nb, 2 * n_state),
                  const(2, nb, n_state),
                  const(n_slab, LANES, 2 * SLAB_STATES),
                  const(n_slab, 2 * SLAB_STATES, LANES),
                  const(n_slab, 1, LANES)],
        out_specs=[pl.BlockSpec((nb, tl, width), lambda i: (0, i, 0)),
                   const(nb, 2 * n_state)],
        out_shape=[jax.ShapeDtypeStruct((nb, seq, width), F32),
                   jax.ShapeDtypeStruct((nb, 2 * n_state), F32)],
        scratch_shapes=[pltpu.VMEM((n_slab, m, LANES), F32),
                        pltpu.VMEM((n_slab, m, LANES), F32),
                        pltpu.VMEM((m, 2 * n_state), F32),
                        pltpu.VMEM((nb, 2 * n_state), F32)],
        compiler_params=pltpu.CompilerParams(
            dimension_semantics=("arbitrary",), vmem_limit_bytes=VMEM_LIMIT),
        name="s5",
    )(h3, h0, lam_b, bmat, cmat, d_slab)


def _lo_lanes():
    return lax.broadcasted_iota(jnp.int32, (1, LANES), 1) < HEAD_DIM


def _head_norm(t, gain2_scaled):
    lo = _lo_lanes()
    sq = t * t
    ss_lo = jnp.sum(jnp.where(lo, sq, 0.0), axis=-1, keepdims=True)
    ss_hi = jnp.sum(jnp.where(lo, 0.0, sq), axis=-1, keepdims=True)
    return t * lax.rsqrt(jnp.where(lo, ss_lo, ss_hi) + HEAD_DIM * EPS) * gain2_scaled


def _pair_blockdiag(t):
    lo = _lo_lanes()
    r = pltpu.roll(t, HEAD_DIM, 1)
    return ((jnp.where(lo, t, 0.0), jnp.where(lo, 0.0, r)),
            (jnp.where(lo, r, 0.0), jnp.where(lo, 0.0, t)))


def _rowmax(parts):
    full, out = None, None
    for p in parts:
        w = p.shape[1]
        for c0 in range(0, w - w % LANES, LANES):
            blk = p[:, c0:c0 + LANES]
            full = blk if full is None else jnp.maximum(full, blk)
        if w % LANES:
            r = jnp.max(p[:, w - w % LANES:], axis=-1, keepdims=True)
            out = r if out is None else jnp.maximum(out, r)
    if full is not None:
        r = jnp.max(full, axis=-1, keepdims=True)
        out = r if out is None else jnp.maximum(out, r)
    return out


def _softmax_numerators(sc, sinks):
    s = sc.shape[1] // 2
    b = (s // LANES) * LANES
    left, mid, right = sc[:, :b], sc[:, b:b + LANES], sc[:, b + LANES:]
    in0 = lax.broadcasted_iota(jnp.int32, (1, LANES), 1) < (s - b)
    m0 = _rowmax([left, jnp.where(in0, mid, sinks[0])])
    m1 = _rowmax([jnp.where(in0, sinks[1], mid), right])
    e = [jnp.exp2(left[:, c:c + LANES] - m0) for c in range(0, b, LANES)]
    e.append(jnp.exp2(mid - jnp.where(in0, m0, m1)))
    e += [jnp.exp2(right[:, c:c + LANES] - m1) for c in range(0, right.shape[1], LANES)]
    sink_share = jnp.exp2(sinks[2] - jnp.where(_lo_lanes(), m0, m1))
    return jnp.concatenate(e, axis=1).astype(BF16), sink_share


def _scores(qn, kbd, bias):
    return lax.dot_general(qn, kbd, (((1,), (1,)), ((), ())), preferred_element_type=F32) + bias


def _weighted_values(e, vbd1, sink_share):
    od = jnp.dot(e, vbd1, preferred_element_type=F32)
    return od[:, :LANES] / (od[:, LANES:] + sink_share)


def _band_attn_body(q_ref, k_ref, v_ref, qg_ref, kg_ref, bias_ref, sink_ref, ones_ref, o_ref, pk_ref, pv_ref,
                    kvbd_ref, qn_ref, sc0_ref, sc1_ref, e0_ref, e1_ref, share0_ref, share1_ref, *, seq, qb):
    j = pl.program_id(1)
    pad = BAND_CHUNKS * CHUNK
    span = pad + CHUNK
    rows = min(512, seq)
    n_pair = GQA_GROUP // 2
    n_chunk = qb // CHUNK
    sc_refs, e_refs, share_refs = (sc0_ref, sc1_ref), (e0_ref, e1_ref), (share0_ref, share1_ref)
    assert n_chunk >= 4 and n_chunk % 2 == 0

    @pl.when(j == 0)
    def _():
        zeros = jnp.zeros((pad, LANES), BF16)
        for a in range(8):
            kvbd_ref[a, 0:pad, :] = zeros

        def fill(i, carry):
            r0 = pl.multiple_of(i * rows, rows)
            kn = _head_norm(k_ref[0, pl.ds(r0, rows), :], kg_ref[...])
            for is_v, t in enumerate((kn, v_ref[0, pl.ds(r0, rows), :])):
                for kv, halves in enumerate(_pair_blockdiag(t)):
                    for tb, half in enumerate(halves):
                        kvbd_ref[4 * is_v + 2 * kv + tb, pl.ds(pad + r0, rows), :] = half.astype(BF16)
            return carry

        lax.fori_loop(0, seq // rows, fill, 0)
        pk_ref[0] = _head_norm(k_ref[0, seq - WINDOW:seq, :], kg_ref[...])
        pv_ref[0] = v_ref[0, seq - WINDOW:seq, :]

    def lanes_of(kv, i):
        return slice((kv * n_pair + i) * LANES, (kv * n_pair + i + 1) * LANES)

    def stage_norm(c):
        q0 = pl.multiple_of(c * CHUNK, CHUNK)
        for kv in range(2):
            for i in range(n_pair):
                qn_ref[c, kv, i * CHUNK:(i + 1) * CHUNK, :] = _head_norm(
                    q_ref[0, pl.ds(q0, CHUNK), lanes_of(kv, i)], qg_ref[...]).astype(BF16)

    def key_rows(c):
        return pl.ds(pl.multiple_of((j * n_chunk + c) * CHUNK, CHUNK), span)

    def stage_scores(c, slot):
        variant = jnp.minimum(j * n_chunk + c, BAND_CHUNKS)
        for kv in range(2):
            kbd = jnp.concatenate([kvbd_ref[2 * kv, key_rows(c), :], kvbd_ref[2 * kv + 1, key_rows(c), :]], axis=0)
            sc_refs[slot][kv] = _scores(qn_ref[c, kv], kbd, bias_ref[variant, kv])

    def stage_exp(slot):
        for kv in range(2):
            e_refs[slot][kv], share_refs[slot][kv] = _softmax_numerators(sc_refs[slot][kv], sink_ref[kv])

    def stage_values(c, slot):
        q0 = pl.multiple_of(c * CHUNK, CHUNK)
        for kv in range(2):
            vbd = jnp.concatenate([kvbd_ref[4 + 2 * kv, key_rows(c), :], kvbd_ref[4 + 2 * kv + 1, key_rows(c), :]], axis=0)
            o = _weighted_values(e_refs[slot][kv], jnp.concatenate([vbd, ones_ref[...]], axis=1), share_refs[slot][kv])
            for i in range(n_pair):
                o_ref[0, pl.ds(q0, CHUNK), lanes_of(kv, i)] = o[i * CHUNK:(i + 1) * CHUNK]

    def iteration(i, parity):
        live = (lambda c: 0 <= c < n_chunk) if isinstance(i, int) else (lambda c: True)
        if live(i - 3):
            stage_values(i - 3, 1 - parity)
        if live(i - 2):
            stage_exp(parity)
        if live(i - 1):
            stage_scores(i - 1, 1 - parity)
        if live(i):
            stage_norm(i)

    for i in range(4):
        iteration(i, i % 2)

    def steady(p, carry):
        iteration(4 + 2 * p, 0)
        iteration(5 + 2 * p, 1)
        return carry

    lax.fori_loop(0, (n_chunk - 4) // 2, steady, 0)
    for i in range(n_chunk, n_chunk + 3):
        iteration(i, i % 2)


def _band_attn(h3, qcol, kcol, vcol, qgain2, kgain2, bias, sinkcol, ones_bd, qb):
    nb, seq, _ = h3.shape
    aw = 2 * GQA_GROUP * HEAD_DIM
    pad = BAND_CHUNKS * CHUNK
    body = functools.partial(_band_attn_body, seq=seq, qb=qb)
    const = lambda *shape: pl.BlockSpec(shape, lambda b, j: (0,) * len(shape))
    return pl.pallas_call(
        body,
        grid=(nb, seq // qb),
        in_specs=[pl.BlockSpec((1, qb, aw), lambda b, j: (b, j, qcol // aw)),
                  pl.BlockSpec((1, seq, LANES), lambda b, j: (b, 0, kcol // LANES)),
                  pl.BlockSpec((1, seq, LANES), lambda b, j: (b, 0, vcol // LANES)),
                  const(1, LANES), const(1, LANES),
                  const(*bias.shape), const(*sinkcol.shape), const(*ones_bd.shape)],
        out_specs=[pl.BlockSpec((1, qb, aw), lambda b, j: (b, j, 0)),
                   pl.BlockSpec((1, WINDOW, LANES), lambda b, j: (b, 0, 0)),
                   pl.BlockSpec((1, WINDOW, LANES), lambda b, j: (b, 0, 0))],
        out_shape=[jax.ShapeDtypeStruct((nb, seq, aw), F32),
                   jax.ShapeDtypeStruct((nb, WINDOW, LANES), F32),
                   jax.ShapeDtypeStruct((nb, WINDOW, LANES), F32)],
        scratch_shapes=[pltpu.VMEM((8, seq + pad, LANES), BF16),
                        pltpu.VMEM((qb // CHUNK, 2, (GQA_GROUP // 2) * CHUNK, LANES), BF16),
                        *[pltpu.VMEM((2, (GQA_GROUP // 2) * CHUNK, 2 * (pad + CHUNK)), F32)] * 2,
                        *[pltpu.VMEM((2, (GQA_GROUP // 2) * CHUNK, 2 * (pad + CHUNK)), BF16)] * 2,
                        *[pltpu.VMEM((2, (GQA_GROUP // 2) * CHUNK, LANES), F32)] * 2],
        compiler_params=pltpu.CompilerParams(
            dimension_semantics=("parallel", "arbitrary"), vmem_limit_bytes=VMEM_LIMIT),
        name="band_attn",
    )(h3, h3, h3, qgain2, kgain2, bias, sinkcol, ones_bd)


def _cached_attn_body(q_ref, k_ref, v_ref, ck_ref, cv_ref, qg_ref, kg_ref, bias_ref, sink_ref, ones_ref,
                      o_ref, sk_ref, sv_ref):
    t = q_ref.shape[1]
    n_pair = GQA_GROUP // 2
    kn = _head_norm(k_ref[0], kg_ref[...])
    vn = v_ref[0]
    sk_ref[0] = kn
    sv_ref[0] = vn
    kbds = _pair_blockdiag(jnp.concatenate([ck_ref[0], kn], axis=0))
    vbds = _pair_blockdiag(jnp.concatenate([cv_ref[0], vn], axis=0))
    for kv in range(2):
        qn = jnp.concatenate(
            [_head_norm(q_ref[0, :, (kv * n_pair + i) * LANES:(kv * n_pair + i + 1) * LANES], qg_ref[...])
             for i in range(n_pair)], axis=0).astype(BF16)
        kbd = jnp.concatenate(kbds[kv], axis=0).astype(BF16)
        vbd1 = jnp.concatenate([jnp.concatenate(vbds[kv], axis=0).astype(BF16), ones_ref[...]], axis=1)
        e, sink_share = _softmax_numerators(_scores(qn, kbd, bias_ref[0, kv]), sink_ref[kv])
        o = _weighted_values(e, vbd1, sink_share)
        for i in range(n_pair):
            o_ref[0, :, (kv * n_pair + i) * LANES:(kv * n_pair + i + 1) * LANES] = o[i * t:(i + 1) * t]


def _cached_attn(h3, qcol, kcol, vcol, cache_k, cache_v, qgain2, kgain2, bias, sinkcol, ones_bd):
    nb, t, _ = h3.shape
    aw = 2 * GQA_GROUP * HEAD_DIM
    r = cache_k.shape[1]
    const = lambda *shape: pl.BlockSpec(shape, lambda b: (0,) * len(shape))
    return pl.pallas_call(
        _cached_attn_body,
        grid=(nb,),
        in_specs=[pl.BlockSpec((1, t, aw), lambda b: (b, 0, qcol // aw)),
                  pl.BlockSpec((1, t, LANES), lambda b: (b, 0, kcol // LANES)),
                  pl.BlockSpec((1, t, LANES), lambda b: (b, 0, vcol // LANES)),
                  pl.BlockSpec((1, r, LANES), lambda b: (b, 0, 0)),
                  pl.BlockSpec((1, r, LANES), lambda b: (b, 0, 0)),
                  const(1, LANES), const(1, LANES),
                  const(*bias.shape), const(*sinkcol.shape), const(*ones_bd.shape)],
        out_specs=[pl.BlockSpec((1, t, aw), lambda b: (b, 0, 0)),
                   pl.BlockSpec((1, t, LANES), lambda b: (b, 0, 0)),
                   pl.BlockSpec((1, t, LANES), lambda b: (b, 0, 0))],
        out_shape=[jax.ShapeDtypeStruct((nb, t, aw), F32),
                   jax.ShapeDtypeStruct((nb, t, LANES), F32),
                   jax.ShapeDtypeStruct((nb, t, LANES), F32)],
        compiler_params=pltpu.CompilerParams(
            dimension_semantics=("parallel",), vmem_limit_bytes=VMEM_LIMIT),
        name="cached_attn",
    )(h3, h3, h3, cache_k, cache_v, qgain2, kgain2, bias, sinkcol, ones_bd)


def _sigmoid(x):
    return 1.0 / (1.0 + jnp.exp(-x))


def _merge_body(x_ref, y_ref, za_ref, o_ref, zb_ref, ga_ref, gb_ref,
                wglu_ref, bglu_ref, woa_ref, wob_ref, wo_ref, out_ref):
    y = y_ref[...]
    g = 0.5 * y * (1.0 + lax.erf(y * (2.0 ** -0.5)))
    gl = jnp.dot(g.astype(BF16), wglu_ref[...], preferred_element_type=F32) + bglu_ref[...]
    za = za_ref[...]
    br_a = g * _sigmoid(gl) * (za * _sigmoid(za))
    zb = zb_ref[...]
    br_b = o_ref[...] * (zb * _sigmoid(zb))
    mixed = (_sigmoid(ga_ref[...]) * jnp.dot(br_a.astype(BF16), woa_ref[...], preferred_element_type=F32)
             + _sigmoid(gb_ref[...]) * jnp.dot(br_b.astype(BF16), wob_ref[...], preferred_element_type=F32))
    out_ref[...] = x_ref[...] + jnp.dot(mixed.astype(BF16), wo_ref[...], preferred_element_type=F32)


def _merge(x2, y2, o2, h2, cols, wglu, bglu, woa, wob, wo, tm):
    n, d = x2.shape
    sw = y2.shape[1]
    aw = o2.shape[1]
    za_col, zb_col, ga_col, gb_col = cols
    row = lambda width, col: pl.BlockSpec((tm, width), lambda i: (i, col // width))
    weight = lambda a: pl.BlockSpec(a.shape, lambda i: (0, 0), pipeline_mode=pl.Buffered(1))
    return pl.pallas_call(
        _merge_body,
        grid=(n // tm,),
        in_specs=[row(d, 0), row(sw, 0), row(sw, za_col), row(aw, 0), row(aw, zb_col),
                  row(d, ga_col), row(d, gb_col),
                  weight(wglu), weight(bglu), weight(woa), weight(wob), weight(wo)],
        out_specs=row(d, 0),
        out_shape=jax.ShapeDtypeStruct((n, d), F32),
        compiler_params=pltpu.CompilerParams(
            dimension_semantics=("parallel",), vmem_limit_bytes=VMEM_LIMIT),
        name="merge",
    )(x2, y2, h2, o2, h2, h2, h2, wglu, bglu, woa, wob, wo)


def _discretize(a_re, a_im, log_dt, b_re, b_im):
    dt = jnp.exp(log_dt)[:, None]
    mag = jnp.exp(a_re * dt)
    ang = a_im * dt
    lam_re = mag * jnp.cos(ang)
    lam_im = mag * jnp.sin(ang)
    den = a_re * a_re + a_im * a_im
    cr = ((lam_re - 1.0) * a_re + lam_im * a_im) / den
    ci = (lam_im * a_re - (lam_re - 1.0) * a_im) / den
    bb_re = cr[..., None] * b_re - ci[..., None] * b_im
    bb_im = cr[..., None] * b_im + ci[..., None] * b_re
    return lam_re, lam_im, bb_re, bb_im


def _slab_blockdiag(t):
    n_slab, g, a, b = t.shape
    eye = jnp.eye(g, dtype=t.dtype)
    return jnp.einsum("sgab,gh->sgahb", t, eye).reshape(n_slab, g * a, g * b)


def _t5_bucket(rel):
    half = N_BUCKETS // 2
    n = -rel
    ret = jnp.where(n < 0, half, 0)
    n = jnp.abs(n)
    max_exact = half // 2
    nf = jnp.maximum(n, 1).astype(F32)
    large = max_exact + (jnp.log(nf / max_exact) / math.log(MAX_DISTANCE / max_exact)
                         * (half - max_exact)).astype(jnp.int32)
    large = jnp.minimum(large, half - 1)
    return ret + jnp.where(n < max_exact, n, large)


def _pair_bias(rel, table):
    t, s = rel.shape
    b = table.astype(F32)[_t5_bucket(rel)] * LOG2E
    b = b.reshape(t, s, 2, GQA_GROUP // 2, 2)
    return jnp.transpose(b, (2, 3, 0, 4, 1)).reshape(2, (GQA_GROUP // 2) * t, 2 * s)


def _pair_sinks(sinks, t):
    sk = jnp.transpose(sinks.astype(F32).reshape(2, GQA_GROUP // 2, 2), (0, 2, 1))[:, :, :, None, None]
    sk = jnp.broadcast_to(sk, (2, 2, GQA_GROUP // 2, t, LANES)).reshape(2, 2, (GQA_GROUP // 2) * t, LANES)
    lo = jnp.arange(LANES) < HEAD_DIM
    return jnp.concatenate([sk, jnp.where(lo, sk[:, 0], sk[:, 1])[:, None]], axis=1) * LOG2E


def _pair_ones(s):
    first = (jnp.arange(2 * s) < s)[:, None]
    lo = (jnp.arange(LANES) < HEAD_DIM)[None, :]
    return (first == lo).astype(BF16)


def kernel(x_prompt, x_sample, cache_k, cache_v, state_ssm_re, state_ssm_im, norm_gain, w_in, ssm_a_re, ssm_a_im, ssm_log_dt, ssm_b_re, ssm_b_im, ssm_c_re, ssm_c_im, ssm_d, w_glu, b_glu, q_gain, k_gain, attn_sinks, rel_bias, w_out_a, w_out_b, w_o):
    nb, seq, d = x_prompt.shape
    db, dseq, _ = x_sample.shape
    n_groups, n_st = ssm_a_re.shape[1:]
    sw = n_groups * SSM_GROUP
    aw = 2 * GQA_GROUP * HEAD_DIM
    kvw = 2 * HEAD_DIM
    n_slab = sw // LANES
    n_state = n_groups * n_st
    l = 0

    s_u, s_za, s_q, s_k, s_v, s_zb, s_ga = sw, 2 * sw, 2 * sw + aw, 2 * sw + aw + kvw, 2 * sw + aw + 2 * kvw, \
        2 * sw + 2 * aw + 2 * kvw, 2 * sw + 2 * aw + 2 * kvw + d
    w = w_in[l]
    w_bf = jnp.concatenate([w[:, :s_u], w[:, s_u:s_za], w[:, s_za:s_q], w[:, s_v:s_zb], w[:, s_zb:s_ga],
                            w[:, s_ga:], w[:, s_q:s_k], w[:, s_k:s_v]], axis=1).astype(BF16)
    c_u, c_za, c_q, c_zb, c_ga, c_gb = 0, sw, 2 * sw, 2 * sw + aw, 2 * sw + 2 * aw, 2 * sw + 2 * aw + d
    c_k = c_gb + d
    c_v = c_k + kvw
    gain = norm_gain[l].astype(F32).reshape(1, d)

    lam_re, lam_im, bb_re, bb_im = _discretize(
        ssm_a_re[l].astype(F32), ssm_a_im[l].astype(F32), ssm_log_dt[l].astype(F32),
        ssm_b_re[l].astype(F32), ssm_b_im[l].astype(F32))
    lam_b = jnp.broadcast_to(jnp.stack([lam_re.reshape(-1), lam_im.reshape(-1)])[:, None, :], (2, nb, n_state))
    slab = lambda t: t.reshape(n_slab, SLAB_GROUPS, *t.shape[1:])
    bmat = jnp.concatenate([_slab_blockdiag(jnp.swapaxes(slab(bb_re), 2, 3)),
                            _slab_blockdiag(jnp.swapaxes(slab(bb_im), 2, 3))], axis=2).astype(BF16)
    cmat = jnp.concatenate([_slab_blockdiag(jnp.swapaxes(slab(ssm_c_re[l].astype(F32)), 2, 3)),
                            _slab_blockdiag(jnp.swapaxes(slab(-ssm_c_im[l].astype(F32)), 2, 3))], axis=1).astype(BF16)
    d_slab = ssm_d[l].astype(F32).reshape(n_slab, 1, LANES)

    qgain2 = jnp.tile(q_gain[l].astype(F32) * (HEAD_DIM ** 0.5 * SCALE * LOG2E), 2).reshape(1, LANES)
    kgain2 = jnp.tile(k_gain[l].astype(F32) * HEAD_DIM ** 0.5, 2).reshape(1, LANES)
    span = (BAND_CHUNKS + 1) * CHUNK
    rel_p = jnp.arange(span)[None, :] - BAND_CHUNKS * CHUNK - jnp.arange(CHUNK)[:, None]
    bias_p = _pair_bias(rel_p, rel_bias)
    key_off = jnp.tile(jnp.arange(span), 2)[None, None, None, :]
    first_valid = ((BAND_CHUNKS - jnp.arange(BAND_CHUNKS + 1)) * CHUNK)[:, None, None, None]
    bias_p = jnp.where(key_off >= first_valid, bias_p[None], -jnp.inf)
    sink_p = _pair_sinks(attn_sinks[l], CHUNK)
    rows = cache_k.shape[2]
    rel_s = jnp.arange(rows + dseq)[None, :] - rows - jnp.arange(dseq)[:, None]
    bias_s = _pair_bias(rel_s, rel_bias)[None]
    sink_s = _pair_sinks(attn_sinks[l], dseq)

    wglu = w_glu[l].astype(BF16)
    bglu = b_glu[l].astype(F32).reshape(1, sw)
    woa = w_out_a[l].astype(BF16)
    wob = w_out_b[l].astype(BF16)
    wo = w_o[l].astype(BF16)
    merge_cols = (c_za, c_zb, c_ga, c_gb)

    xp2 = x_prompt.reshape(nb * seq, d)
    hp = _inproj(xp2, gain, w_bf, tm=min(1024, nb * seq), tn=1408)
    hp3 = hp.reshape(nb, seq, -1)
    yp, hfin_p = _s5(hp3, jnp.zeros((nb, 2 * n_state), F32), lam_b, bmat, cmat, d_slab, tl=min(64, seq))
    op, pk, pv = _band_attn(hp3, c_q, c_k, c_v, qgain2, kgain2, bias_p, sink_p, _pair_ones(span), qb=min(1024, seq))
    y_p = _merge(xp2, yp.reshape(nb * seq, sw), op.reshape(nb * seq, aw), hp, merge_cols,
                 wglu, bglu, woa, wob, wo, tm=256).reshape(nb, seq, d)

    xs2 = x_sample.reshape(db * dseq, d)
    hs = _inproj(xs2, gain, w_bf, tm=db * dseq, tn=768)
    hs3 = hs.reshape(db, dseq, -1)
    h0 = jnp.concatenate([state_ssm_re[l].reshape(db, n_state), state_ssm_im[l].reshape(db, n_state)],
                         axis=1).astype(F32)
    ys, hfin_s = _s5(hs3, h0, lam_b, bmat, cmat, d_slab, tl=dseq)
    ck = cache_k[l].reshape(db, rows, kvw).astype(F32)
    cv = cache_v[l].reshape(db, rows, kvw).astype(F32)
    os_, sk, sv = _cached_attn(hs3, c_q, c_k, c_v, ck, cv, qgain2, kgain2, bias_s, sink_s, _pair_ones(rows + dseq))
    y_s = _merge(xs2, ys.reshape(db * dseq, sw), os_.reshape(db * dseq, aw), hs, merge_cols,
                 wglu, bglu, woa, wob, wo, tm=db * dseq).reshape(db, dseq, d)

    st = lambda h, nbb: (h[:, :n_state].reshape(1, nbb, n_groups, n_st), h[:, n_state:].reshape(1, nbb, n_groups, n_st))
    p_re, p_im = st(hfin_p, nb)
    s_re, s_im = st(hfin_s, db)
    kvshape = lambda a: a.reshape(1, a.shape[0], a.shape[1], 2, HEAD_DIM)
    return (y_p, y_s, p_re, p_im, kvshape(pk), kvshape(pv), s_re, s_im, kvshape(sk), kvshape(sv))
```

```python
import functools
import math

import jax
import jax.numpy as jnp
from jax import lax
from jax.experimental import pallas as pl
from jax.experimental.pallas import tpu as pltpu

F32 = jnp.float32
BF16 = jnp.bfloat16

LANES = 128
SUBLANES = 8
VMEM_LIMIT = 56 * 1024 * 1024

CHUNK = 64
HEAD_DIM = 64
GQA_GROUP = 8
SSM_GROUP = 16
SSM_STATE = 64
WINDOW = 128
BAND_CHUNKS = 2
N_BUCKETS = 32
MAX_DISTANCE = 128
EPS = 1e-6
SCALE = HEAD_DIM ** -0.5
LOG2E = math.log2(math.e)
SLAB_GROUPS = LANES // SSM_GROUP
SLAB_STATES = SLAB_GROUPS * SSM_STATE
SCAN_LANES = 1024


def _inproj_body(x_ref, g_ref, w_ref, o_ref, xn_ref):
    @pl.when(pl.program_id(1) == 0)
    def _():
        x = x_ref[...]
        ms = jnp.mean(x * x, axis=-1, keepdims=True)
        xn_ref[...] = (x * lax.rsqrt(ms + EPS) * g_ref[...]).astype(BF16)

    o_ref[...] = jnp.dot(xn_ref[...], w_ref[...], preferred_element_type=F32)


def _inproj(x2, gain, w_bf, tm, tn):
    n, d = x2.shape
    width = w_bf.shape[1]
    return pl.pallas_call(
        _inproj_body,
        grid=(n // tm, width // tn),
        in_specs=[pl.BlockSpec((tm, d), lambda i, j: (i, 0)),
                  pl.BlockSpec((1, d), lambda i, j: (0, 0)),
                  pl.BlockSpec((d, tn), lambda i, j: (0, j))],
        out_specs=pl.BlockSpec((tm, tn), lambda i, j: (i, j)),
        out_shape=jax.ShapeDtypeStruct((n, width), F32),
        scratch_shapes=[pltpu.VMEM((tm, d), BF16)],
        compiler_params=pltpu.CompilerParams(
            dimension_semantics=("parallel", "arbitrary"), vmem_limit_bytes=VMEM_LIMIT),
        name="inproj",
    )(x2, gain, w_bf)


def _s5_body(u_ref, h0_ref, lam_ref, bmat_ref, cmat_ref, d_ref, y_ref, hout_ref,
             utb_ref, ytb_ref, bu_ref, hst_ref, *, nb, tl, n_slab, n_state):
    m = tl * nb

    @pl.when(pl.program_id(0) == 0)
    def _():
        hst_ref[...] = h0_ref[...]

    for s in range(n_slab):
        for b in range(nb):
            utb_ref[s, pl.ds(b, tl, stride=nb), :] = u_ref[b, :, s * LANES:(s + 1) * LANES]

    for s in range(n_slab):
        bu = jnp.dot(utb_ref[s].astype(BF16), bmat_ref[s], preferred_element_type=F32)
        bu_ref[:, s * SLAB_STATES:(s + 1) * SLAB_STATES] = bu[:, :SLAB_STATES]
        bu_ref[:, n_state + s * SLAB_STATES:n_state + (s + 1) * SLAB_STATES] = bu[:, SLAB_STATES:]

    for c in range(n_state // SCAN_LANES):
        re = slice(c * SCAN_LANES, (c + 1) * SCAN_LANES)
        im = slice(n_state + c * SCAN_LANES, n_state + (c + 1) * SCAN_LANES)
        lr = lam_ref[0, :, re]
        li = lam_ref[1, :, re]

        def step(t, carry, re=re, im=im, lr=lr, li=li):
            hr, hi = carry
            r0 = pl.multiple_of(t * nb, SUBLANES)
            nhr = lr * hr - li * hi + bu_ref[pl.ds(r0, nb), re]
            nhi = lr * hi + li * hr + bu_ref[pl.ds(r0, nb), im]
            bu_ref[pl.ds(r0, nb), re] = nhr
            bu_ref[pl.ds(r0, nb), im] = nhi
            return nhr, nhi

        hr, hi = lax.fori_loop(0, tl, step, (hst_ref[:, re], hst_ref[:, im]), unroll=2)
        hst_ref[:, re] = hr
        hst_ref[:, im] = hi

    hout_ref[...] = hst_ref[...]

    for s in range(n_slab):
        hre = bu_ref[:, s * SLAB_STATES:(s + 1) * SLAB_STATES].astype(BF16)
        him = bu_ref[:, n_state + s * SLAB_STATES:n_state + (s + 1) * SLAB_STATES].astype(BF16)
        y = (jnp.dot(hre, cmat_ref[s, :SLAB_STATES, :], preferred_element_type=F32)
             + jnp.dot(him, cmat_ref[s, SLAB_STATES:, :], preferred_element_type=F32))
        ytb_ref[s] = y + d_ref[s] * utb_ref[s]

    for s in range(n_slab):
        for b in range(nb):
            y_ref[b, :, s * LANES:(s + 1) * LANES] = ytb_ref[s, pl.ds(b, tl, stride=nb), :]


def _s5(h3, h0, lam_b, bmat, cmat, d_slab, tl):
    nb, seq, _ = h3.shape
    n_slab = bmat.shape[0]
    width = n_slab * LANES
    n_state = n_slab * SLAB_STATES
    m = tl * nb
    body = functools.partial(_s5_body, nb=nb, tl=tl, n_slab=n_slab, n_state=n_state)
    const = lambda *shape: pl.BlockSpec(shape, lambda i: (0,) * len(shape))
    return pl.pallas_call(
        body,
        grid=(seq // tl,),
        in_specs=[pl.BlockSpec((nb, tl, width), lambda i: (0, i, 0)),
                  const(nb, 2 * n_state),
                  const(2, nb, n_state),
                  const(n_slab, LANES, 2 * SLAB_STATES),
                  const(n_slab, 2 * SLAB_STATES, LANES),
                  const(n_slab, 1, LANES)],
        out_specs=[pl.BlockSpec((nb, tl, width), lambda i: (0, i, 0)),
                   const(nb, 2 * n_state)],
        out_shape=[jax.ShapeDtypeStruct((nb, seq, width), F32),
                   jax.ShapeDtypeStruct((nb, 2 * n_state), F32)],
        scratch_shapes=[pltpu.VMEM((n_slab, m, LANES), F32),
                        pltpu.VMEM((n_slab, m, LANES), F32),
                        pltpu.VMEM((m, 2 * n_state), F32),
                        pltpu.VMEM((nb, 2 * n_state), F32)],
        compiler_params=pltpu.CompilerParams(
            dimension_semantics=("arbitrary",), vmem_limit_bytes=VMEM_LIMIT),
        name="s5",
    )(h3, h0, lam_b, bmat, cmat, d_slab)


def _lo_lanes():
    return lax.broadcasted_iota(jnp.int32, (1, LANES), 1) < HEAD_DIM


def _head_norm(t, gain2_scaled):
    lo = _lo_lanes()
    sq = t * t
    ss_lo = jnp.sum(jnp.where(lo, sq, 0.0), axis=-1, keepdims=True)
    ss_hi = jnp.sum(jnp.where(lo, 0.0, sq), axis=-1, keepdims=True)
    return t * lax.rsqrt(jnp.where(lo, ss_lo, ss_hi) + HEAD_DIM * EPS) * gain2_scaled


def _pair_blockdiag(t):
    lo = _lo_lanes()
    r = pltpu.roll(t, HEAD_DIM, 1)
    return ((jnp.where(lo, t, 0.0), jnp.where(lo, 0.0, r)),
            (jnp.where(lo, r, 0.0), jnp.where(lo, 0.0, t)))


def _rowmax(parts):
    full, out = None, None
    for p in parts:
        w = p.shape[1]
        for c0 in range(0, w - w % LANES, LANES):
            blk = p[:, c0:c0 + LANES]
            full = blk if full is None else jnp.maximum(full, blk)
        if w % LANES:
            r = jnp.max(p[:, w - w % LANES:], axis=-1, keepdims=True)
            out = r if out is None else jnp.maximum(out, r)
    if full is not None:
        r = jnp.max(full, axis=-1, keepdims=True)
        out = r if out is None else jnp.maximum(out, r)
    return out


def _softmax_numerators(sc, sinks):
    s = sc.shape[1] // 2
    b = (s // LANES) * LANES
    left, mid, right = sc[:, :b], sc[:, b:b + LANES], sc[:, b + LANES:]
    in0 = lax.broadcasted_iota(jnp.int32, (1, LANES), 1) < (s - b)
    m0 = _rowmax([left, jnp.where(in0, mid, sinks[0])])
    m1 = _rowmax([jnp.where(in0, sinks[1], mid), right])
    e = [jnp.exp2(left[:, c:c + LANES] - m0) for c in range(0, b, LANES)]
    e.append(jnp.exp2(mid - jnp.where(in0, m0, m1)))
    e += [jnp.exp2(right[:, c:c + LANES] - m1) for c in range(0, right.shape[1], LANES)]
    sink_share = jnp.exp2(sinks[2] - jnp.where(_lo_lanes(), m0, m1))
    return jnp.concatenate(e, axis=1).astype(BF16), sink_share


def _scores(qn, kbd, bias):
    return lax.dot_general(qn, kbd, (((1,), (1,)), ((), ())), preferred_element_type=F32) + bias


def _weighted_values(e, vbd1, sink_share):
    od = jnp.dot(e, vbd1, preferred_element_type=F32)
    return od[:, :LANES] / (od[:, LANES:] + sink_share)


def _band_attn_body(q_ref, k_ref, v_ref, qg_ref, kg_ref, bias_ref, sink_ref, ones_ref, o_ref, pk_ref, pv_ref,
                    kvbd_ref, qn_ref, sc0_ref, sc1_ref, e0_ref, e1_ref, share0_ref, share1_ref, *, seq, qb):
    j = pl.program_id(1)
    pad = BAND_CHUNKS * CHUNK
    span = pad + CHUNK
    rows = min(512, seq)
    n_pair = GQA_GROUP // 2
    n_chunk = qb // CHUNK
    sc_refs, e_refs, share_refs = (sc0_ref, sc1_ref), (e0_ref, e1_ref), (share0_ref, share1_ref)
    assert n_chunk >= 4 and n_chunk % 2 == 0

    @pl.when(j == 0)
    def _():
        zeros = jnp.zeros((pad, LANES), BF16)
        for a in range(8):
            kvbd_ref[a, 0:pad, :] = zeros

        def fill(i, carry):
            r0 = pl.multiple_of(i * rows, rows)
            kn = _head_norm(k_ref[0, pl.ds(r0, rows), :], kg_ref[...])
            for is_v, t in enumerate((kn, v_ref[0, pl.ds(r0, rows), :])):
                for kv, halves in enumerate(_pair_blockdiag(t)):
                    for tb, half in enumerate(halves):
                        kvbd_ref[4 * is_v + 2 * kv + tb, pl.ds(pad + r0, rows), :] = half.astype(BF16)
            return carry

        lax.fori_loop(0, seq // rows, fill, 0)
        pk_ref[0] = _head_norm(k_ref[0, seq - WINDOW:seq, :], kg_ref[...])
        pv_ref[0] = v_ref[0, seq - WINDOW:seq, :]

    def lanes_of(kv, i):
        return slice((kv * n_pair + i) * LANES, (kv * n_pair + i + 1) * LANES)

    def stage_norm(c):
        q0 = pl.multiple_of(c * CHUNK, CHUNK)
        for kv in range(2):
            for i in range(n_pair):
                qn_ref[c, kv, i * CHUNK:(i + 1) * CHUNK, :] = _head_norm(
                    q_ref[0, pl.ds(q0, CHUNK), lanes_of(kv, i)], qg_ref[...]).astype(BF16)

    def key_rows(c):
        return pl.ds(pl.multiple_of((j * n_chunk + c) * CHUNK, CHUNK), span)

    def stage_scores(c, slot):
        variant = jnp.minimum(j * n_chunk + c, BAND_CHUNKS)
        for kv in range(2):
            kbd = jnp.concatenate([kvbd_ref[2 * kv, key_rows(c), :], kvbd_ref[2 * kv + 1, key_rows(c), :]], axis=0)
            sc_refs[slot][kv] = _scores(qn_ref[c, kv], kbd, bias_ref[variant, kv])

    def stage_exp(slot):
        for kv in range(2):
            e_refs[slot][kv], share_refs[slot][kv] = _softmax_numerators(sc_refs[slot][kv], sink_ref[kv])

    def stage_values(c, slot):
        q0 = pl.multiple_of(c * CHUNK, CHUNK)
        for kv in range(2):
            vbd = jnp.concatenate([kvbd_ref[4 + 2 * kv, key_rows(c), :], kvbd_ref[4 + 2 * kv + 1, key_rows(c), :]], axis=0)
            o = _weighted_values(e_refs[slot][kv], jnp.concatenate([vbd, ones_ref[...]], axis=1), share_refs[slot][kv])
            for i in range(n_pair):
                o_ref[0, pl.ds(q0, CHUNK), lanes_of(kv, i)] = o[i * CHUNK:(i + 1) * CHUNK]

    def iteration(i, parity):
        live = (lambda c: 0 <= c < n_chunk) if isinstance(i, int) else (lambda c: True)
        if live(i - 3):
            stage_values(i - 3, 1 - parity)
        if live(i - 2):
            stage_exp(parity)
        if live(i - 1):
            stage_scores(i - 1, 1 - parity)
        if live(i):
            stage_norm(i)

    for i in range(4):
        iteration(i, i % 2)

    def steady(p, carry):
        iteration(4 + 2 * p, 0)
        iteration(5 + 2 * p, 1)
        return carry

    lax.fori_loop(0, (n_chunk - 4) // 2, steady, 0)
    for i in range(n_chunk, n_chunk + 3):
        iteration(i, i % 2)


def _band_attn(h3, qcol, kcol, vcol, qgain2, kgain2, bias, sinkcol, ones_bd, qb):
    nb, seq, _ = h3.shape
    aw = 2 * GQA_GROUP * HEAD_DIM
    pad = BAND_CHUNKS * CHUNK
    body = functools.partial(_band_attn_body, seq=seq, qb=qb)
    const = lambda *shape: pl.BlockSpec(shape, lambda b, j: (0,) * len(shape))
    return pl.pallas_call(
        body,
        grid=(nb, seq // qb),
        in_specs=[pl.BlockSpec((1, qb, aw), lambda b, j: (b, j, qcol // aw)),
                  pl.BlockSpec((1, seq, LANES), lambda b, j: (b, 0, kcol // LANES)),
                  pl.BlockSpec((1, seq, LANES), lambda b, j: (b, 0, vcol // LANES)),
                  const(1, LANES), const(1, LANES),
                  const(*bias.shape), const(*sinkcol.shape), const(*ones_bd.shape)],
        out_specs=[pl.BlockSpec((1, qb, aw), lambda b, j: (b, j, 0)),
                   pl.BlockSpec((1, WINDOW, LANES), lambda b, j: (b, 0, 0)),
                   pl.BlockSpec((1, WINDOW, LANES), lambda b, j: (b, 0, 0))],
        out_shape=[jax.ShapeDtypeStruct((nb, seq, aw), F32),
                   jax.ShapeDtypeStruct((nb, WINDOW, LANES), F32),
                   jax.ShapeDtypeStruct((nb, WINDOW, LANES), F32)],
        scratch_shapes=[pltpu.VMEM((8, seq + pad, LANES), BF16),
                        pltpu.VMEM((qb // CHUNK, 2, (GQA_GROUP // 2) * CHUNK, LANES), BF16),
                        *[pltpu.VMEM((2, (GQA_GROUP // 2) * CHUNK, 2 * (pad + CHUNK)), F32)] * 2,
                        *[pltpu.VMEM((2, (GQA_GROUP // 2) * CHUNK, 2 * (pad + CHUNK)), BF16)] * 2,
                        *[pltpu.VMEM((2, (GQA_GROUP // 2) * CHUNK, LANES), F32)] * 2],
        compiler_params=pltpu.CompilerParams(
            dimension_semantics=("parallel", "arbitrary"), vmem_limit_bytes=VMEM_LIMIT),
        name="band_attn",
    )(h3, h3, h3, qgain2, kgain2, bias, sinkcol, ones_bd)


def _cached_attn_body(q_ref, k_ref, v_ref, ck_ref, cv_ref, qg_ref, kg_ref, bias_ref, sink_ref, ones_ref,
                      o_ref, sk_ref, sv_ref):
    t = q_ref.shape[1]
    n_pair = GQA_GROUP // 2
    kn = _head_norm(k_ref[0], kg_ref[...])
    vn = v_ref[0]
    sk_ref[0] = kn
    sv_ref[0] = vn
    kbds = _pair_blockdiag(jnp.concatenate([ck_ref[0], kn], axis=0))
    vbds = _pair_blockdiag(jnp.concatenate([cv_ref[0], vn], axis=0))
    for kv in range(2):
        qn = jnp.concatenate(
            [_head_norm(q_ref[0, :, (kv * n_pair + i) * LANES:(kv * n_pair + i + 1) * LANES], qg_ref[...])
             for i in range(n_pair)], axis=0).astype(BF16)
        kbd = jnp.concatenate(kbds[kv], axis=0).astype(BF16)
        vbd1 = jnp.concatenate([jnp.concatenate(vbds[kv], axis=0).astype(BF16), ones_ref[...]], axis=1)
        e, sink_share = _softmax_numerators(_scores(qn, kbd, bias_ref[0, kv]), sink_ref[kv])
        o = _weighted_values(e, vbd1, sink_share)
        for i in range(n_pair):
            o_ref[0, :, (kv * n_pair + i) * LANES:(kv * n_pair + i + 1) * LANES] = o[i * t:(i + 1) * t]


def _cached_attn(h3, qcol, kcol, vcol, cache_k, cache_v, qgain2, kgain2, bias, sinkcol, ones_bd):
    nb, t, _ = h3.shape
    aw = 2 * GQA_GROUP * HEAD_DIM
    r = cache_k.shape[1]
    const = lambda *shape: pl.BlockSpec(shape, lambda b: (0,) * len(shape))
    return pl.pallas_call(
        _cached_attn_body,
        grid=(nb,),
        in_specs=[pl.BlockSpec((1, t, aw), lambda b: (b, 0, qcol // aw)),
                  pl.BlockSpec((1, t, LANES), lambda b: (b, 0, kcol // LANES)),
                  pl.BlockSpec((1, t, LANES), lambda b: (b, 0, vcol // LANES)),
                  pl.BlockSpec((1, r, LANES), lambda b: (b, 0, 0)),
                  pl.BlockSpec((1, r, LANES), lambda b: (b, 0, 0)),
                  const(1, LANES), const(1, LANES),
                  const(*bias.shape), const(*sinkcol.shape), const(*ones_bd.shape)],
        out_specs=[pl.BlockSpec((1, t, aw), lambda b: (b, 0, 0)),
                   pl.BlockSpec((1, t, LANES), lambda b: (b, 0, 0)),
                   pl.BlockSpec((1, t, LANES), lambda b: (b, 0, 0))],
        out_shape=[jax.ShapeDtypeStruct((nb, t, aw), F32),
                   jax.ShapeDtypeStruct((nb, t, LANES), F32),
                   jax.ShapeDtypeStruct((nb, t, LANES), F32)],
        compiler_params=pltpu.CompilerParams(
            dimension_semantics=("parallel",), vmem_limit_bytes=VMEM_LIMIT),
        name="cached_attn",
    )(h3, h3, h3, cache_k, cache_v, qgain2, kgain2, bias, sinkcol, ones_bd)


def _sigmoid(x):
    return 1.0 / (1.0 + jnp.exp(-x))


def _merge_body(x_ref, y_ref, za_ref, o_ref, zb_ref, ga_ref, gb_ref,
                wglu_ref, bglu_ref, woa_ref, wob_ref, wo_ref, out_ref):
    y = y_ref[...]
    g = 0.5 * y * (1.0 + lax.erf(y * (2.0 ** -0.5)))
    gl = jnp.dot(g.astype(BF16), wglu_ref[...], preferred_element_type=F32) + bglu_ref[...]
    za = za_ref[...]
    br_a = g * _sigmoid(gl) * (za * _sigmoid(za))
    zb = zb_ref[...]
    br_b = o_ref[...] * (zb * _sigmoid(zb))
    mixed = (_sigmoid(ga_ref[...]) * jnp.dot(br_a.astype(BF16), woa_ref[...], preferred_element_type=F32)
             + _sigmoid(gb_ref[...]) * jnp.dot(br_b.astype(BF16), wob_ref[...], preferred_element_type=F32))
    out_ref[...] = x_ref[...] + jnp.dot(mixed.astype(BF16), wo_ref[...], preferred_element_type=F32)


def _merge(x2, y2, o2, h2, cols, wglu, bglu, woa, wob, wo, tm):
    n, d = x2.shape
    sw = y2.shape[1]
    aw = o2.shape[1]
    za_col, zb_col, ga_col, gb_col = cols
    row = lambda width, col: pl.BlockSpec((pl.Element(tm), pl.Element(width)), lambda i: (i * tm, col))
    weight = lambda a: pl.BlockSpec(a.shape, lambda i: (0, 0), pipeline_mode=pl.Buffered(1))
    return pl.pallas_call(
        _merge_body,
        grid=(n // tm,),
        in_specs=[row(d, 0), row(sw, 0), row(sw, za_col), row(aw, 0), row(aw, zb_col),
                  row(d, ga_col), row(d, gb_col),
                  weight(wglu), weight(bglu), weight(woa), weight(wob), weight(wo)],
        out_specs=row(d, 0),
        out_shape=jax.ShapeDtypeStruct((n, d), F32),
        compiler_params=pltpu.CompilerParams(
            dimension_semantics=("parallel",), vmem_limit_bytes=VMEM_LIMIT),
        name="merge",
    )(x2, y2, h2, o2, h2, h2, h2, wglu, bglu, woa, wob, wo)


def _discretize(a_re, a_im, log_dt, b_re, b_im):
    dt = jnp.exp(log_dt)[:, None]
    mag = jnp.exp(a_re * dt)
    ang = a_im * dt
    lam_re = mag * jnp.cos(ang)
    lam_im = mag * jnp.sin(ang)
    den = a_re * a_re + a_im * a_im
    cr = ((lam_re - 1.0) * a_re + lam_im * a_im) / den
    ci = (lam_im * a_re - (lam_re - 1.0) * a_im) / den
    bb_re = cr[..., None] * b_re - ci[..., None] * b_im
    bb_im = cr[..., None] * b_im + ci[..., None] * b_re
    return lam_re, lam_im, bb_re, bb_im


def _slab_blockdiag(t):
    n_slab, g, a, b = t.shape
    eye = jnp.eye(g, dtype=t.dtype)
    return jnp.einsum("sgab,gh->sgahb", t, eye).reshape(n_slab, g * a, g * b)


def _t5_bucket(rel):
    half = N_BUCKETS // 2
    n = -rel
    ret = jnp.where(n < 0, half, 0)
    n = jnp.abs(n)
    max_exact = half // 2
    nf = jnp.maximum(n, 1).astype(F32)
    large = max_exact + (jnp.log(nf / max_exact) / math.log(MAX_DISTANCE / max_exact)
                         * (half - max_exact)).astype(jnp.int32)
    large = jnp.minimum(large, half - 1)
    return ret + jnp.where(n < max_exact, n, large)


def _pair_bias(rel, table):
    t, s = rel.shape
    onehot = (_t5_bucket(rel)[..., None] == jnp.arange(N_BUCKETS)).astype(F32)
    b = jnp.einsum("tsn,nh->tsh", onehot, table.astype(F32) * LOG2E, precision=lax.Precision.HIGHEST)
    b = b.reshape(t, s, 2, GQA_GROUP // 2, 2)
    return jnp.transpose(b, (2, 3, 0, 4, 1)).reshape(2, (GQA_GROUP // 2) * t, 2 * s)


def _pair_sinks(sinks, t):
    sk = jnp.transpose(sinks.astype(F32).reshape(2, GQA_GROUP // 2, 2), (0, 2, 1))[:, :, :, None, None]
    sk = jnp.broadcast_to(sk, (2, 2, GQA_GROUP // 2, t, LANES)).reshape(2, 2, (GQA_GROUP // 2) * t, LANES)
    lo = jnp.arange(LANES) < HEAD_DIM
    return jnp.concatenate([sk, jnp.where(lo, sk[:, 0], sk[:, 1])[:, None]], axis=1) * LOG2E


def _pair_ones(s):
    first = (jnp.arange(2 * s) < s)[:, None]
    lo = (jnp.arange(LANES) < HEAD_DIM)[None, :]
    return (first == lo).astype(BF16)


def kernel(x_prompt, x_sample, cache_k, cache_v, state_ssm_re, state_ssm_im, norm_gain, w_in, ssm_a_re, ssm_a_im, ssm_log_dt, ssm_b_re, ssm_b_im, ssm_c_re, ssm_c_im, ssm_d, w_glu, b_glu, q_gain, k_gain, attn_sinks, rel_bias, w_out_a, w_out_b, w_o):
    nb, seq, d = x_prompt.shape
    db, dseq, _ = x_sample.shape
    n_groups, n_st = ssm_a_re.shape[1:]
    sw = n_groups * SSM_GROUP
    aw = 2 * GQA_GROUP * HEAD_DIM
    kvw = 2 * HEAD_DIM
    n_slab = sw // LANES
    n_state = n_groups * n_st
    l = 0

    c_u, c_za, c_q = 0, sw, 2 * sw
    c_k = c_q + aw
    c_v = c_k + kvw
    c_zb = c_v + kvw
    c_ga = c_zb + aw
    c_gb = c_ga + d
    w_bf = w_in[l].astype(BF16)
    gain = norm_gain[l].astype(F32).reshape(1, d)

    lam_re, lam_im, bb_re, bb_im = _discretize(
        ssm_a_re[l].astype(F32), ssm_a_im[l].astype(F32), ssm_log_dt[l].astype(F32),
        ssm_b_re[l].astype(F32), ssm_b_im[l].astype(F32))
    lam_b = jnp.broadcast_to(jnp.stack([lam_re.reshape(-1), lam_im.reshape(-1)])[:, None, :], (2, nb, n_state))
    slab = lambda t: t.reshape(n_slab, SLAB_GROUPS, *t.shape[1:])
    bmat = jnp.concatenate([_slab_blockdiag(jnp.swapaxes(slab(bb_re), 2, 3)),
                            _slab_blockdiag(jnp.swapaxes(slab(bb_im), 2, 3))], axis=2).astype(BF16)
    cmat = jnp.concatenate([_slab_blockdiag(jnp.swapaxes(slab(ssm_c_re[l].astype(F32)), 2, 3)),
                            _slab_blockdiag(jnp.swapaxes(slab(-ssm_c_im[l].astype(F32)), 2, 3))], axis=1).astype(BF16)
    d_slab = ssm_d[l].astype(F32).reshape(n_slab, 1, LANES)

    qgain2 = jnp.tile(q_gain[l].astype(F32) * (HEAD_DIM ** 0.5 * SCALE * LOG2E), 2).reshape(1, LANES)
    kgain2 = jnp.tile(k_gain[l].astype(F32) * HEAD_DIM ** 0.5, 2).reshape(1, LANES)
    span = (BAND_CHUNKS + 1) * CHUNK
    rel_p = jnp.arange(span)[None, :] - BAND_CHUNKS * CHUNK - jnp.arange(CHUNK)[:, None]
    bias_p = _pair_bias(rel_p, rel_bias)
    key_off = jnp.tile(jnp.arange(span), 2)[None, None, None, :]
    first_valid = ((BAND_CHUNKS - jnp.arange(BAND_CHUNKS + 1)) * CHUNK)[:, None, None, None]
    bias_p = jnp.where(key_off >= first_valid, bias_p[None], -jnp.inf)
    sink_p = _pair_sinks(attn_sinks[l], CHUNK)
    rows = cache_k.shape[2]
    rel_s = jnp.arange(rows + dseq)[None, :] - rows - jnp.arange(dseq)[:, None]
    bias_s = _pair_bias(rel_s, rel_bias)[None]
    sink_s = _pair_sinks(attn_sinks[l], dseq)

    wglu = w_glu[l].astype(BF16)
    bglu = b_glu[l].astype(F32).reshape(1, sw)
    woa = w_out_a[l].astype(BF16)
    wob = w_out_b[l].astype(BF16)
    wo = w_o[l].astype(BF16)
    merge_cols = (c_za, c_zb, c_ga, c_gb)

    xp2 = x_prompt.reshape(nb * seq, d)
    hp = _inproj(xp2, gain, w_bf, tm=min(512, nb * seq), tn=2816)
    hp3 = hp.reshape(nb, seq, -1)
    yp, hfin_p = _s5(hp3, jnp.zeros((nb, 2 * n_state), F32), lam_b, bmat, cmat, d_slab, tl=min(64, seq))
    op, pk, pv = _band_attn(hp3, c_q, c_k, c_v, qgain2, kgain2, bias_p, sink_p, _pair_ones(span), qb=min(1024, seq))
    y_p = _merge(xp2, yp.reshape(nb * seq, sw), op.reshape(nb * seq, aw), hp, merge_cols,
                 wglu, bglu, woa, wob, wo, tm=256).reshape(nb, seq, d)

    xs2 = x_sample.reshape(db * dseq, d)
    hs = _inproj(xs2, gain, w_bf, tm=db * dseq, tn=768)
    hs3 = hs.reshape(db, dseq, -1)
    h0 = jnp.concatenate([state_ssm_re[l].reshape(db, n_state), state_ssm_im[l].reshape(db, n_state)],
                         axis=1).astype(F32)
    ys, hfin_s = _s5(hs3, h0, lam_b, bmat, cmat, d_slab, tl=dseq)
    ck = cache_k[l].reshape(db, rows, kvw).astype(F32)
    cv = cache_v[l].reshape(db, rows, kvw).astype(F32)
    os_, sk, sv = _cached_attn(hs3, c_q, c_k, c_v, ck, cv, qgain2, kgain2, bias_s, sink_s, _pair_ones(rows + dseq))
    y_s = _merge(xs2, ys.reshape(db * dseq, sw), os_.reshape(db * dseq, aw), hs, merge_cols,
                 wglu, bglu, woa, wob, wo, tm=db * dseq).reshape(db, dseq, d)

    st = lambda h, nbb: (h[:, :n_state].reshape(1, nbb, n_groups, n_st), h[:, n_state:].reshape(1, nbb, n_groups, n_st))
    p_re, p_im = st(hfin_p, nb)
    s_re, s_im = st(hfin_s, db)
    kvshape = lambda a: a.reshape(1, a.shape[0], a.shape[1], 2, HEAD_DIM)
    return (y_p, y_s, p_re, p_im, kvshape(pk), kvshape(pv), s_re, s_im, kvshape(sk), kvshape(sv))
```

```python
import functools
import math

import jax
import jax.numpy as jnp
from jax import lax
from jax.experimental import pallas as pl
from jax.experimental.pallas import tpu as pltpu

F32 = jnp.float32
BF16 = jnp.bfloat16

LANES = 128
SUBLANES = 8
VMEM_LIMIT = 56 * 1024 * 1024

CHUNK = 64
HEAD_DIM = 64
GQA_GROUP = 8
SSM_GROUP = 16
SSM_STATE = 64
WINDOW = 128
BAND_CHUNKS = 2
N_BUCKETS = 32
MAX_DISTANCE = 128
EPS = 1e-6
SCALE = HEAD_DIM ** -0.5
LOG2E = math.log2(math.e)
SLAB_GROUPS = LANES // SSM_GROUP
SLAB_STATES = SLAB_GROUPS * SSM_STATE
SCAN_LANES = 1024


def _inproj_body(x_ref, g_ref, w_ref, o_ref, *, n_split):
    x = x_ref[...]
    ms = jnp.mean(x * x, axis=-1, keepdims=True)
    xn = (x * lax.rsqrt(ms + EPS) * g_ref[...]).astype(BF16)
    tn = w_ref.shape[1] // n_split
    for c in range(n_split):
        o_ref[:, c * tn:(c + 1) * tn] = jnp.dot(
            xn, w_ref[:, c * tn:(c + 1) * tn], preferred_element_type=F32).astype(BF16)


def _inproj(x2, gain, w_bf, tm, n_split):
    n, d = x2.shape
    width = w_bf.shape[1]
    return pl.pallas_call(
        functools.partial(_inproj_body, n_split=n_split),
        grid=(n // tm,),
        in_specs=[pl.BlockSpec((tm, d), lambda i: (i, 0)),
                  pl.BlockSpec((1, d), lambda i: (0, 0)),
                  pl.BlockSpec((d, width), lambda i: (0, 0), pipeline_mode=pl.Buffered(1))],
        out_specs=pl.BlockSpec((tm, width), lambda i: (i, 0)),
        out_shape=jax.ShapeDtypeStruct((n, width), BF16),
        compiler_params=pltpu.CompilerParams(
            dimension_semantics=("parallel",), vmem_limit_bytes=VMEM_LIMIT),
        name="inproj",
    )(x2, gain, w_bf)


def _s5_body(u_ref, h0_ref, lam_ref, bmat_ref, cmat_ref, d_ref, y_ref, hout_ref,
             utb_ref, ytb_ref, bu_ref, hst_ref, *, nb, tl, n_slab, n_state, n_sub):
    sub_t = tl // n_sub
    sub_m = sub_t * nb
    n_pass = n_state // SCAN_LANES

    @pl.when(pl.program_id(0) == 0)
    def _():
        hst_ref[...] = h0_ref[...]

    for s in range(n_slab):
        for b in range(nb):
            utb_ref[s, pl.ds(b, tl, stride=nb), :] = u_ref[b, :, s * LANES:(s + 1) * LANES].astype(F32)

    def re_cols(s, width=SLAB_STATES):
        return slice(s * width, (s + 1) * width)

    def im_cols(s, width=SLAB_STATES):
        return slice(n_state + s * width, n_state + (s + 1) * width)

    def project_in(k):
        rows = slice(k * sub_m, (k + 1) * sub_m)
        for s in range(n_slab):
            bu = jnp.dot(utb_ref[s, rows, :].astype(BF16), bmat_ref[s], preferred_element_type=F32)
            bu_ref[rows, re_cols(s)] = bu[:, :SLAB_STATES]
            bu_ref[rows, im_cols(s)] = bu[:, SLAB_STATES:]

    def scan(k, state):
        out = []
        for c, (hr, hi) in enumerate(state):
            re, im = re_cols(c, SCAN_LANES), im_cols(c, SCAN_LANES)
            lr = lam_ref[0, :, re]
            li = lam_ref[1, :, re]
            for t in range(k * sub_t, (k + 1) * sub_t):
                r = slice(t * nb, (t + 1) * nb)
                hr, hi = lr * hr - li * hi + bu_ref[r, re], lr * hi + li * hr + bu_ref[r, im]
                bu_ref[r, re] = hr
                bu_ref[r, im] = hi
            out.append((hr, hi))
        return out

    def project_out(k):
        rows = slice(k * sub_m, (k + 1) * sub_m)
        for s in range(n_slab):
            y = (jnp.dot(bu_ref[rows, re_cols(s)].astype(BF16), cmat_ref[s, :SLAB_STATES, :], preferred_element_type=F32)
                 + jnp.dot(bu_ref[rows, im_cols(s)].astype(BF16), cmat_ref[s, SLAB_STATES:, :], preferred_element_type=F32))
            ytb_ref[s, rows, :] = y + d_ref[s] * utb_ref[s, rows, :]

    state = [(hst_ref[:, re_cols(c, SCAN_LANES)], hst_ref[:, im_cols(c, SCAN_LANES)]) for c in range(n_pass)]
    project_in(0)
    for k in range(n_sub):
        if k + 1 < n_sub:
            project_in(k + 1)
        state = scan(k, state)
        if k >= 1:
            project_out(k - 1)
    project_out(n_sub - 1)
    for c, (hr, hi) in enumerate(state):
        hst_ref[:, re_cols(c, SCAN_LANES)] = hr
        hst_ref[:, im_cols(c, SCAN_LANES)] = hi
    hout_ref[...] = hst_ref[...]

    for s in range(n_slab):
        for b in range(nb):
            y_ref[b, :, s * LANES:(s + 1) * LANES] = ytb_ref[s, pl.ds(b, tl, stride=nb), :].astype(BF16)


def _s5(h3, h0, lam_b, bmat, cmat, d_slab, tl, n_sub):
    nb, seq, _ = h3.shape
    n_slab = bmat.shape[0]
    width = n_slab * LANES
    n_state = n_slab * SLAB_STATES
    m = tl * nb
    body = functools.partial(_s5_body, nb=nb, tl=tl, n_slab=n_slab, n_state=n_state, n_sub=n_sub)
    const = lambda *shape: pl.BlockSpec(shape, lambda i: (0,) * len(shape))
    return pl.pallas_call(
        body,
        grid=(seq // tl,),
        in_specs=[pl.BlockSpec((nb, tl, width), lambda i: (0, i, 0)),
                  const(nb, 2 * n_state),
                  const(2, nb, n_state),
                  const(n_slab, LANES, 2 * SLAB_STATES),
                  const(n_slab, 2 * SLAB_STATES, LANES),
                  const(n_slab, 1, LANES)],
        out_specs=[pl.BlockSpec((nb, tl, width), lambda i: (0, i, 0)),
                   const(nb, 2 * n_state)],
        out_shape=[jax.ShapeDtypeStruct((nb, seq, width), BF16),
                   jax.ShapeDtypeStruct((nb, 2 * n_state), F32)],
        scratch_shapes=[pltpu.VMEM((n_slab, m, LANES), F32),
                        pltpu.VMEM((n_slab, m, LANES), F32),
                        pltpu.VMEM((m, 2 * n_state), F32),
                        pltpu.VMEM((nb, 2 * n_state), F32)],
        compiler_params=pltpu.CompilerParams(
            dimension_semantics=("arbitrary",), vmem_limit_bytes=VMEM_LIMIT),
        name="s5",
    )(h3, h0, lam_b, bmat, cmat, d_slab)


def _lo_lanes():
    return lax.broadcasted_iota(jnp.int32, (1, LANES), 1) < HEAD_DIM


def _head_norm(t, gain2_scaled):
    lo = _lo_lanes()
    sq = t * t
    ss_lo = jnp.sum(jnp.where(lo, sq, 0.0), axis=-1, keepdims=True)
    ss_hi = jnp.sum(jnp.where(lo, 0.0, sq), axis=-1, keepdims=True)
    return t * lax.rsqrt(jnp.where(lo, ss_lo, ss_hi) + HEAD_DIM * EPS) * gain2_scaled


def _pair_blockdiag(t):
    lo = _lo_lanes()
    r = pltpu.roll(t, HEAD_DIM, 1)
    return ((jnp.where(lo, t, 0.0), jnp.where(lo, 0.0, r)),
            (jnp.where(lo, r, 0.0), jnp.where(lo, 0.0, t)))


def _rowmax(parts):
    full, out = None, None
    for p in parts:
        w = p.shape[1]
        for c0 in range(0, w - w % LANES, LANES):
            blk = p[:, c0:c0 + LANES]
            full = blk if full is None else jnp.maximum(full, blk)
        if w % LANES:
            r = jnp.max(p[:, w - w % LANES:], axis=-1, keepdims=True)
            out = r if out is None else jnp.maximum(out, r)
    if full is not None:
        r = jnp.max(full, axis=-1, keepdims=True)
        out = r if out is None else jnp.maximum(out, r)
    return out


def _softmax_numerators(sc, sinks):
    s = sc.shape[1] // 2
    b = (s // LANES) * LANES
    left, mid, right = sc[:, :b], sc[:, b:b + LANES], sc[:, b + LANES:]
    in0 = lax.broadcasted_iota(jnp.int32, (1, LANES), 1) < (s - b)
    m0 = _rowmax([left, jnp.where(in0, mid, sinks[0])])
    m1 = _rowmax([jnp.where(in0, sinks[1], mid), right])
    e = [jnp.exp2(left[:, c:c + LANES] - m0) for c in range(0, b, LANES)]
    e.append(jnp.exp2(mid - jnp.where(in0, m0, m1)))
    e += [jnp.exp2(right[:, c:c + LANES] - m1) for c in range(0, right.shape[1], LANES)]
    sink_share = jnp.exp2(sinks[2] - jnp.where(_lo_lanes(), m0, m1))
    return jnp.concatenate(e, axis=1).astype(BF16), sink_share


def _scores(qn, kbd, bias):
    return lax.dot_general(qn, kbd, (((1,), (1,)), ((), ())), preferred_element_type=F32) + bias


def _weighted_values(e, vbd1, sink_share):
    od = jnp.dot(e, vbd1, preferred_element_type=F32)
    return od[:, :LANES] / (od[:, LANES:] + sink_share)


def _band_attn_body(q_ref, k_ref, v_ref, qg_ref, kg_ref, bias_ref, sink_ref, ones_ref, o_ref, pk_ref, pv_ref,
                    kvbd_ref, qn_ref, sc0_ref, sc1_ref, e0_ref, e1_ref, share0_ref, share1_ref, *, seq, qb):
    j = pl.program_id(1)
    pad = BAND_CHUNKS * CHUNK
    span = pad + CHUNK
    rows = min(512, seq)
    n_pair = GQA_GROUP // 2
    n_chunk = qb // CHUNK
    sc_refs, e_refs, share_refs = (sc0_ref, sc1_ref), (e0_ref, e1_ref), (share0_ref, share1_ref)
    assert n_chunk >= 4 and n_chunk % 2 == 0

    @pl.when(j == 0)
    def _():
        zeros = jnp.zeros((pad, LANES), BF16)
        for a in range(8):
            kvbd_ref[a, 0:pad, :] = zeros

        def fill(i, carry):
            r0 = pl.multiple_of(i * rows, rows)
            kn = _head_norm(k_ref[0, pl.ds(r0, rows), :].astype(F32), kg_ref[...])
            for is_v, t in enumerate((kn, v_ref[0, pl.ds(r0, rows), :].astype(F32))):
                for kv, halves in enumerate(_pair_blockdiag(t)):
                    for tb, half in enumerate(halves):
                        kvbd_ref[4 * is_v + 2 * kv + tb, pl.ds(pad + r0, rows), :] = half.astype(BF16)
            return carry

        lax.fori_loop(0, seq // rows, fill, 0)
        pk_ref[0] = _head_norm(k_ref[0, seq - WINDOW:seq, :].astype(F32), kg_ref[...])
        pv_ref[0] = v_ref[0, seq - WINDOW:seq, :].astype(F32)

    def lanes_of(kv, i):
        return slice((kv * n_pair + i) * LANES, (kv * n_pair + i + 1) * LANES)

    def stage_norm(c):
        q0 = pl.multiple_of(c * CHUNK, CHUNK)
        for kv in range(2):
            for i in range(n_pair):
                qn_ref[c, kv, i * CHUNK:(i + 1) * CHUNK, :] = _head_norm(
                    q_ref[0, pl.ds(q0, CHUNK), lanes_of(kv, i)].astype(F32), qg_ref[...]).astype(BF16)

    def key_rows(c):
        return pl.ds(pl.multiple_of((j * n_chunk + c) * CHUNK, CHUNK), span)

    def stage_scores(c, slot):
        variant = jnp.minimum(j * n_chunk + c, BAND_CHUNKS)
        for kv in range(2):
            kbd = jnp.concatenate([kvbd_ref[2 * kv, key_rows(c), :], kvbd_ref[2 * kv + 1, key_rows(c), :]], axis=0)
            sc_refs[slot][kv] = _scores(qn_ref[c, kv], kbd, bias_ref[variant, kv])

    def stage_exp(slot):
        for kv in range(2):
            e_refs[slot][kv], share_refs[slot][kv] = _softmax_numerators(sc_refs[slot][kv], sink_ref[kv])

    def stage_values(c, slot):
        q0 = pl.multiple_of(c * CHUNK, CHUNK)
        for kv in range(2):
            vbd = jnp.concatenate([kvbd_ref[4 + 2 * kv, key_rows(c), :], kvbd_ref[4 + 2 * kv + 1, key_rows(c), :]], axis=0)
            o = _weighted_values(e_refs[slot][kv], jnp.concatenate([vbd, ones_ref[...]], axis=1), share_refs[slot][kv])
            for i in range(n_pair):
                o_ref[0, pl.ds(q0, CHUNK), lanes_of(kv, i)] = o[i * CHUNK:(i + 1) * CHUNK].astype(BF16)

    def iteration(i, parity):
        live = (lambda c: 0 <= c < n_chunk) if isinstance(i, int) else (lambda c: True)
        if live(i - 3):
            stage_values(i - 3, 1 - parity)
        if live(i - 2):
            stage_exp(parity)
        if live(i - 1):
            stage_scores(i - 1, 1 - parity)
        if live(i):
            stage_norm(i)

    for i in range(4):
        iteration(i, i % 2)

    def steady(p, carry):
        iteration(4 + 2 * p, 0)
        iteration(5 + 2 * p, 1)
        return carry

    lax.fori_loop(0, (n_chunk - 4) // 2, steady, 0)
    for i in range(n_chunk, n_chunk + 3):
        iteration(i, i % 2)


def _band_attn(h3, qcol, kcol, vcol, qgain2, kgain2, bias, sinkcol, ones_bd, qb):
    nb, seq, _ = h3.shape
    aw = 2 * GQA_GROUP * HEAD_DIM
    pad = BAND_CHUNKS * CHUNK
    body = functools.partial(_band_attn_body, seq=seq, qb=qb)
    const = lambda *shape: pl.BlockSpec(shape, lambda b, j: (0,) * len(shape))
    return pl.pallas_call(
        body,
        grid=(nb, seq // qb),
        in_specs=[pl.BlockSpec((1, qb, aw), lambda b, j: (b, j, qcol // aw)),
                  pl.BlockSpec((1, seq, LANES), lambda b, j: (b, 0, kcol // LANES)),
                  pl.BlockSpec((1, seq, LANES), lambda b, j: (b, 0, vcol // LANES)),
                  const(1, LANES), const(1, LANES),
                  const(*bias.shape), const(*sinkcol.shape), const(*ones_bd.shape)],
        out_specs=[pl.BlockSpec((1, qb, aw), lambda b, j: (b, j, 0)),
                   pl.BlockSpec((1, WINDOW, LANES), lambda b, j: (b, 0, 0)),
                   pl.BlockSpec((1, WINDOW, LANES), lambda b, j: (b, 0, 0))],
        out_shape=[jax.ShapeDtypeStruct((nb, seq, aw), BF16),
                   jax.ShapeDtypeStruct((nb, WINDOW, LANES), F32),
                   jax.ShapeDtypeStruct((nb, WINDOW, LANES), F32)],
        scratch_shapes=[pltpu.VMEM((8, seq + pad, LANES), BF16),
                        pltpu.VMEM((qb // CHUNK, 2, (GQA_GROUP // 2) * CHUNK, LANES), BF16),
                        *[pltpu.VMEM((2, (GQA_GROUP // 2) * CHUNK, 2 * (pad + CHUNK)), F32)] * 2,
                        *[pltpu.VMEM((2, (GQA_GROUP // 2) * CHUNK, 2 * (pad + CHUNK)), BF16)] * 2,
                        *[pltpu.VMEM((2, (GQA_GROUP // 2) * CHUNK, LANES), F32)] * 2],
        compiler_params=pltpu.CompilerParams(
            dimension_semantics=("parallel", "arbitrary"), vmem_limit_bytes=VMEM_LIMIT),
        name="band_attn",
    )(h3, h3, h3, qgain2, kgain2, bias, sinkcol, ones_bd)


def _cached_attn_body(q_ref, k_ref, v_ref, ck_ref, cv_ref, qg_ref, kg_ref, bias_ref, sink_ref, ones_ref,
                      o_ref, sk_ref, sv_ref):
    t = q_ref.shape[1]
    n_pair = GQA_GROUP // 2
    kn = _head_norm(k_ref[0].astype(F32), kg_ref[...])
    vn = v_ref[0].astype(F32)
    sk_ref[0] = kn
    sv_ref[0] = vn
    kbds = _pair_blockdiag(jnp.concatenate([ck_ref[0], kn], axis=0))
    vbds = _pair_blockdiag(jnp.concatenate([cv_ref[0], vn], axis=0))
    for kv in range(2):
        qn = jnp.concatenate(
            [_head_norm(q_ref[0, :, (kv * n_pair + i) * LANES:(kv * n_pair + i + 1) * LANES].astype(F32), qg_ref[...])
             for i in range(n_pair)], axis=0).astype(BF16)
        kbd = jnp.concatenate(kbds[kv], axis=0).astype(BF16)
        vbd1 = jnp.concatenate([jnp.concatenate(vbds[kv], axis=0).astype(BF16), ones_ref[...]], axis=1)
        e, sink_share = _softmax_numerators(_scores(qn, kbd, bias_ref[0, kv]), sink_ref[kv])
        o = _weighted_values(e, vbd1, sink_share)
        for i in range(n_pair):
            o_ref[0, :, (kv * n_pair + i) * LANES:(kv * n_pair + i + 1) * LANES] = o[i * t:(i + 1) * t].astype(BF16)


def _cached_attn(h3, qcol, kcol, vcol, cache_k, cache_v, qgain2, kgain2, bias, sinkcol, ones_bd):
    nb, t, _ = h3.shape
    aw = 2 * GQA_GROUP * HEAD_DIM
    r = cache_k.shape[1]
    const = lambda *shape: pl.BlockSpec(shape, lambda b: (0,) * len(shape))
    return pl.pallas_call(
        _cached_attn_body,
        grid=(nb,),
        in_specs=[pl.BlockSpec((1, t, aw), lambda b: (b, 0, qcol // aw)),
                  pl.BlockSpec((1, t, LANES), lambda b: (b, 0, kcol // LANES)),
                  pl.BlockSpec((1, t, LANES), lambda b: (b, 0, vcol // LANES)),
                  pl.BlockSpec((1, r, LANES), lambda b: (b, 0, 0)),
                  pl.BlockSpec((1, r, LANES), lambda b: (b, 0, 0)),
                  const(1, LANES), const(1, LANES),
                  const(*bias.shape), const(*sinkcol.shape), const(*ones_bd.shape)],
        out_specs=[pl.BlockSpec((1, t, aw), lambda b: (b, 0, 0)),
                   pl.BlockSpec((1, t, LANES), lambda b: (b, 0, 0)),
                   pl.BlockSpec((1, t, LANES), lambda b: (b, 0, 0))],
        out_shape=[jax.ShapeDtypeStruct((nb, t, aw), BF16),
                   jax.ShapeDtypeStruct((nb, t, LANES), F32),
                   jax.ShapeDtypeStruct((nb, t, LANES), F32)],
        compiler_params=pltpu.CompilerParams(
            dimension_semantics=("parallel",), vmem_limit_bytes=VMEM_LIMIT),
        name="cached_attn",
    )(h3, h3, h3, cache_k, cache_v, qgain2, kgain2, bias, sinkcol, ones_bd)


def _sigmoid(x):
    return 1.0 / (1.0 + jnp.exp(-x))


def _merge_body(x_ref, y_ref, za_ref, o_ref, zb_ref, ga_ref, gb_ref,
                wglu_ref, bglu_ref, woa_ref, wob_ref, wo_ref, out_ref):
    y = y_ref[...].astype(F32)
    g = 0.5 * y * (1.0 + lax.erf(y * (2.0 ** -0.5)))
    gl = jnp.dot(g.astype(BF16), wglu_ref[...], preferred_element_type=F32) + bglu_ref[...]
    za = za_ref[...].astype(F32)
    br_a = g * _sigmoid(gl) * (za * _sigmoid(za))
    zb = zb_ref[...].astype(F32)
    br_b = o_ref[...].astype(F32) * (zb * _sigmoid(zb))
    mixed = (_sigmoid(ga_ref[...].astype(F32)) * jnp.dot(br_a.astype(BF16), woa_ref[...], preferred_element_type=F32)
             + _sigmoid(gb_ref[...].astype(F32)) * jnp.dot(br_b.astype(BF16), wob_ref[...], preferred_element_type=F32))
    out_ref[...] = x_ref[...] + jnp.dot(mixed.astype(BF16), wo_ref[...], preferred_element_type=F32)


def _merge(x2, y2, o2, h2, cols, wglu, bglu, woa, wob, wo, tm):
    n, d = x2.shape
    sw = y2.shape[1]
    aw = o2.shape[1]
    za_col, zb_col, ga_col, gb_col = cols
    row = lambda width, col: pl.BlockSpec((pl.Element(tm), pl.Element(width)), lambda i: (i * tm, col))
    weight = lambda a: pl.BlockSpec(a.shape, lambda i: (0, 0), pipeline_mode=pl.Buffered(1))
    return pl.pallas_call(
        _merge_body,
        grid=(n // tm,),
        in_specs=[row(d, 0), row(sw, 0), row(sw, za_col), row(aw, 0), row(aw, zb_col),
                  row(d, ga_col), row(d, gb_col),
                  weight(wglu), weight(bglu), weight(woa), weight(wob), weight(wo)],
        out_specs=row(d, 0),
        out_shape=jax.ShapeDtypeStruct((n, d), F32),
        compiler_params=pltpu.CompilerParams(
            dimension_semantics=("parallel",), vmem_limit_bytes=VMEM_LIMIT),
        name="merge",
    )(x2, y2, h2, o2, h2, h2, h2, wglu, bglu, woa, wob, wo)


def _discretize(a_re, a_im, log_dt, b_re, b_im):
    dt = jnp.exp(log_dt)[:, None]
    mag = jnp.exp(a_re * dt)
    ang = a_im * dt
    lam_re = mag * jnp.cos(ang)
    lam_im = mag * jnp.sin(ang)
    den = a_re * a_re + a_im * a_im
    cr = ((lam_re - 1.0) * a_re + lam_im * a_im) / den
    ci = (lam_im * a_re - (lam_re - 1.0) * a_im) / den
    bb_re = cr[..., None] * b_re - ci[..., None] * b_im
    bb_im = cr[..., None] * b_im + ci[..., None] * b_re
    return lam_re, lam_im, bb_re, bb_im


def _slab_blockdiag(t):
    n_slab, g, a, b = t.shape
    eye = jnp.eye(g, dtype=t.dtype)
    return jnp.einsum("sgab,gh->sgahb", t, eye).reshape(n_slab, g * a, g * b)


def _t5_bucket(rel):
    half = N_BUCKETS // 2
    n = -rel
    ret = jnp.where(n < 0, half, 0)
    n = jnp.abs(n)
    max_exact = half // 2
    nf = jnp.maximum(n, 1).astype(F32)
    large = max_exact + (jnp.log(nf / max_exact) / math.log(MAX_DISTANCE / max_exact)
                         * (half - max_exact)).astype(jnp.int32)
    large = jnp.minimum(large, half - 1)
    return ret + jnp.where(n < max_exact, n, large)


def _pair_bias(rel, table):
    t, s = rel.shape
    onehot = (_t5_bucket(rel)[..., None] == jnp.arange(N_BUCKETS)).astype(F32)
    b = jnp.einsum("tsn,nh->tsh", onehot, table.astype(F32) * LOG2E, precision=lax.Precision.HIGHEST)
    b = b.reshape(t, s, 2, GQA_GROUP // 2, 2)
    return jnp.transpose(b, (2, 3, 0, 4, 1)).reshape(2, (GQA_GROUP // 2) * t, 2 * s)


def _pair_sinks(sinks, t):
    sk = jnp.transpose(sinks.astype(F32).reshape(2, GQA_GROUP // 2, 2), (0, 2, 1))[:, :, :, None, None]
    sk = jnp.broadcast_to(sk, (2, 2, GQA_GROUP // 2, t, LANES)).reshape(2, 2, (GQA_GROUP // 2) * t, LANES)
    lo = jnp.arange(LANES) < HEAD_DIM
    return jnp.concatenate([sk, jnp.where(lo, sk[:, 0], sk[:, 1])[:, None]], axis=1) * LOG2E


def _pair_ones(s):
    first = (jnp.arange(2 * s) < s)[:, None]
    lo = (jnp.arange(LANES) < HEAD_DIM)[None, :]
    return (first == lo).astype(BF16)


def kernel(x_prompt, x_sample, cache_k, cache_v, state_ssm_re, state_ssm_im, norm_gain, w_in, ssm_a_re, ssm_a_im, ssm_log_dt, ssm_b_re, ssm_b_im, ssm_c_re, ssm_c_im, ssm_d, w_glu, b_glu, q_gain, k_gain, attn_sinks, rel_bias, w_out_a, w_out_b, w_o):
    nb, seq, d = x_prompt.shape
    db, dseq, _ = x_sample.shape
    n_groups, n_st = ssm_a_re.shape[1:]
    sw = n_groups * SSM_GROUP
    aw = 2 * GQA_GROUP * HEAD_DIM
    kvw = 2 * HEAD_DIM
    n_slab = sw // LANES
    n_state = n_groups * n_st
    l = 0

    c_u, c_za, c_q = 0, sw, 2 * sw
    c_k = c_q + aw
    c_v = c_k + kvw
    c_zb = c_v + kvw
    c_ga = c_zb + aw
    c_gb = c_ga + d
    w_bf = w_in[l].astype(BF16)
    gain = norm_gain[l].astype(F32).reshape(1, d)

    lam_re, lam_im, bb_re, bb_im = _discretize(
        ssm_a_re[l].astype(F32), ssm_a_im[l].astype(F32), ssm_log_dt[l].astype(F32),
        ssm_b_re[l].astype(F32), ssm_b_im[l].astype(F32))
    lam_b = jnp.broadcast_to(jnp.stack([lam_re.reshape(-1), lam_im.reshape(-1)])[:, None, :], (2, nb, n_state))
    slab = lambda t: t.reshape(n_slab, SLAB_GROUPS, *t.shape[1:])
    bmat = jnp.concatenate([_slab_blockdiag(jnp.swapaxes(slab(bb_re), 2, 3)),
                            _slab_blockdiag(jnp.swapaxes(slab(bb_im), 2, 3))], axis=2).astype(BF16)
    cmat = jnp.concatenate([_slab_blockdiag(jnp.swapaxes(slab(ssm_c_re[l].astype(F32)), 2, 3)),
                            _slab_blockdiag(jnp.swapaxes(slab(-ssm_c_im[l].astype(F32)), 2, 3))], axis=1).astype(BF16)
    d_slab = ssm_d[l].astype(F32).reshape(n_slab, 1, LANES)

    qgain2 = jnp.tile(q_gain[l].astype(F32) * (HEAD_DIM ** 0.5 * SCALE * LOG2E), 2).reshape(1, LANES)
    kgain2 = jnp.tile(k_gain[l].astype(F32) * HEAD_DIM ** 0.5, 2).reshape(1, LANES)
    span = (BAND_CHUNKS + 1) * CHUNK
    rel_p = jnp.arange(span)[None, :] - BAND_CHUNKS * CHUNK - jnp.arange(CHUNK)[:, None]
    bias_p = _pair_bias(rel_p, rel_bias)
    key_off = jnp.tile(jnp.arange(span), 2)[None, None, None, :]
    first_valid = ((BAND_CHUNKS - jnp.arange(BAND_CHUNKS + 1)) * CHUNK)[:, None, None, None]
    bias_p = jnp.where(key_off >= first_valid, bias_p[None], -jnp.inf)
    sink_p = _pair_sinks(attn_sinks[l], CHUNK)
    rows = cache_k.shape[2]
    rel_s = jnp.arange(rows + dseq)[None, :] - rows - jnp.arange(dseq)[:, None]
    bias_s = _pair_bias(rel_s, rel_bias)[None]
    sink_s = _pair_sinks(attn_sinks[l], dseq)

    wglu = w_glu[l].astype(BF16)
    bglu = b_glu[l].astype(F32).reshape(1, sw)
    woa = w_out_a[l].astype(BF16)
    wob = w_out_b[l].astype(BF16)
    wo = w_o[l].astype(BF16)
    merge_cols = (c_za, c_zb, c_ga, c_gb)

    xp2 = x_prompt.reshape(nb * seq, d)
    hp = _inproj(xp2, gain, w_bf, tm=min(256, nb * seq), n_split=3)
    hp3 = hp.reshape(nb, seq, -1)
    yp, hfin_p = _s5(hp3, jnp.zeros((nb, 2 * n_state), F32), lam_b, bmat, cmat, d_slab, tl=min(64, seq), n_sub=2)
    op, pk, pv = _band_attn(hp3, c_q, c_k, c_v, qgain2, kgain2, bias_p, sink_p, _pair_ones(span), qb=min(1024, seq))
    y_p = _merge(xp2, yp.reshape(nb * seq, sw), op.reshape(nb * seq, aw), hp, merge_cols,
                 wglu, bglu, woa, wob, wo, tm=256).reshape(nb, seq, d)

    xs2 = x_sample.reshape(db * dseq, d)
    hs = _inproj(xs2, gain, w_bf, tm=db * dseq, n_split=3)
    hs3 = hs.reshape(db, dseq, -1)
    h0 = jnp.concatenate([state_ssm_re[l].reshape(db, n_state), state_ssm_im[l].reshape(db, n_state)],
                         axis=1).astype(F32)
    ys, hfin_s = _s5(hs3, h0, lam_b, bmat, cmat, d_slab, tl=dseq, n_sub=1)
    ck = cache_k[l].reshape(db, rows, kvw).astype(F32)
    cv = cache_v[l].reshape(db, rows, kvw).astype(F32)
    os_, sk, sv = _cached_attn(hs3, c_q, c_k, c_v, ck, cv, qgain2, kgain2, bias_s, sink_s, _pair_ones(rows + dseq))
    y_s = _merge(xs2, ys.reshape(db * dseq, sw), os_.reshape(db * dseq, aw), hs, merge_cols,
                 wglu, bglu, woa, wob, wo, tm=db * dseq).reshape(db, dseq, d)

    st = lambda h, nbb: (h[:, :n_state].reshape(1, nbb, n_groups, n_st), h[:, n_state:].reshape(1, nbb, n_groups, n_st))
    p_re, p_im = st(hfin_p, nb)
    s_re, s_im = st(hfin_s, db)
    kvshape = lambda a: a.reshape(1, a.shape[0], a.shape[1], 2, HEAD_DIM)
    return (y_p, y_s, p_re, p_im, kvshape(pk), kvshape(pv), s_re, s_im, kvshape(sk), kvshape(sv))
```

```python
import functools
import math

import jax
import jax.numpy as jnp
from jax import lax
from jax.experimental import pallas as pl
from jax.experimental.pallas import tpu as pltpu

F32 = jnp.float32
BF16 = jnp.bfloat16

LANES = 128
SUBLANES = 8
VMEM_LIMIT = 56 * 1024 * 1024

CHUNK = 64
HEAD_DIM = 64
GQA_GROUP = 8
SSM_GROUP = 16
SSM_STATE = 64
WINDOW = 128
BAND_CHUNKS = 2
N_BUCKETS = 32
MAX_DISTANCE = 128
EPS = 1e-6
SCALE = HEAD_DIM ** -0.5
LOG2E = math.log2(math.e)
SLAB_GROUPS = LANES // SSM_GROUP
SLAB_STATES = SLAB_GROUPS * SSM_STATE
SCAN_LANES = 1024


def _inproj_body(x_ref, g_ref, w_ref, o_ref, *, n_split):
    x = x_ref[...]
    ms = jnp.mean(x * x, axis=-1, keepdims=True)
    xn = (x * lax.rsqrt(ms + EPS) * g_ref[...]).astype(BF16)
    tn = w_ref.shape[1] // n_split
    for c in range(n_split):
        o_ref[:, c * tn:(c + 1) * tn] = jnp.dot(
            xn, w_ref[:, c * tn:(c + 1) * tn], preferred_element_type=F32).astype(BF16)


def _inproj(x2, gain, w_bf, tm, n_split):
    n, d = x2.shape
    width = w_bf.shape[1]
    return pl.pallas_call(
        functools.partial(_inproj_body, n_split=n_split),
        grid=(n // tm,),
        in_specs=[pl.BlockSpec((tm, d), lambda i: (i, 0)),
                  pl.BlockSpec((1, d), lambda i: (0, 0)),
                  pl.BlockSpec((d, width), lambda i: (0, 0), pipeline_mode=pl.Buffered(1))],
        out_specs=pl.BlockSpec((tm, width), lambda i: (i, 0)),
        out_shape=jax.ShapeDtypeStruct((n, width), BF16),
        compiler_params=pltpu.CompilerParams(
            dimension_semantics=("parallel",), vmem_limit_bytes=VMEM_LIMIT),
        name="inproj",
    )(x2, gain, w_bf)


def _s5_body(u_ref, h0_ref, lam_ref, bmat_ref, cmat_ref, d_ref, y_ref, hout_ref,
             utb_ref, ytb_ref, bu_ref, hst_ref, *, nb, tl, n_slab, n_state, n_sub):
    sub_t = tl // n_sub
    sub_m = sub_t * nb
    n_pass = n_state // SCAN_LANES

    @pl.when(pl.program_id(0) == 0)
    def _():
        hst_ref[...] = h0_ref[...]

    for s in range(n_slab):
        for b in range(nb):
            utb_ref[s, pl.ds(b, tl, stride=nb), :] = u_ref[b, :, s * LANES:(s + 1) * LANES].astype(F32)

    def re_cols(s, width=SLAB_STATES):
        return slice(s * width, (s + 1) * width)

    def im_cols(s, width=SLAB_STATES):
        return slice(n_state + s * width, n_state + (s + 1) * width)

    def project_in(k):
        rows = slice(k * sub_m, (k + 1) * sub_m)
        for s in range(n_slab):
            bu = jnp.dot(utb_ref[s, rows, :].astype(BF16), bmat_ref[s], preferred_element_type=F32)
            bu_ref[rows, re_cols(s)] = bu[:, :SLAB_STATES]
            bu_ref[rows, im_cols(s)] = bu[:, SLAB_STATES:]

    def scan(k, state):
        out = []
        for c, (hr, hi) in enumerate(state):
            re, im = re_cols(c, SCAN_LANES), im_cols(c, SCAN_LANES)
            lr = lam_ref[0, :, re]
            li = lam_ref[1, :, re]
            for t in range(k * sub_t, (k + 1) * sub_t):
                r = slice(t * nb, (t + 1) * nb)
                hr, hi = lr * hr - li * hi + bu_ref[r, re], lr * hi + li * hr + bu_ref[r, im]
                bu_ref[r, re] = hr
                bu_ref[r, im] = hi
            out.append((hr, hi))
        return out

    def project_out(k):
        rows = slice(k * sub_m, (k + 1) * sub_m)
        for s in range(n_slab):
            y = (jnp.dot(bu_ref[rows, re_cols(s)].astype(BF16), cmat_ref[s, :SLAB_STATES, :], preferred_element_type=F32)
                 + jnp.dot(bu_ref[rows, im_cols(s)].astype(BF16), cmat_ref[s, SLAB_STATES:, :], preferred_element_type=F32))
            ytb_ref[s, rows, :] = y + d_ref[s] * utb_ref[s, rows, :]

    state = [(hst_ref[:, re_cols(c, SCAN_LANES)], hst_ref[:, im_cols(c, SCAN_LANES)]) for c in range(n_pass)]
    project_in(0)
    for k in range(n_sub):
        if k + 1 < n_sub:
            project_in(k + 1)
        state = scan(k, state)
        if k >= 1:
            project_out(k - 1)
    project_out(n_sub - 1)
    for c, (hr, hi) in enumerate(state):
        hst_ref[:, re_cols(c, SCAN_LANES)] = hr
        hst_ref[:, im_cols(c, SCAN_LANES)] = hi
    hout_ref[...] = hst_ref[...]

    for s in range(n_slab):
        for b in range(nb):
            y_ref[b, :, s * LANES:(s + 1) * LANES] = ytb_ref[s, pl.ds(b, tl, stride=nb), :].astype(BF16)


def _s5(h3, h0, lam_b, bmat, cmat, d_slab, tl, n_sub):
    nb, seq, _ = h3.shape
    n_slab = bmat.shape[0]
    width = n_slab * LANES
    n_state = n_slab * SLAB_STATES
    m = tl * nb
    body = functools.partial(_s5_body, nb=nb, tl=tl, n_slab=n_slab, n_state=n_state, n_sub=n_sub)
    const = lambda *shape: pl.BlockSpec(shape, lambda i: (0,) * len(shape))
    return pl.pallas_call(
        body,
        grid=(seq // tl,),
        in_specs=[pl.BlockSpec((nb, tl, width), lambda i: (0, i, 0)),
                  const(nb, 2 * n_state),
                  const(2, nb, n_state),
                  const(n_slab, LANES, 2 * SLAB_STATES),
                  const(n_slab, 2 * SLAB_STATES, LANES),
                  const(n_slab, 1, LANES)],
        out_specs=[pl.BlockSpec((nb, tl, width), lambda i: (0, i, 0)),
                   const(nb, 2 * n_state)],
        out_shape=[jax.ShapeDtypeStruct((nb, seq, width), BF16),
                   jax.ShapeDtypeStruct((nb, 2 * n_state), F32)],
        scratch_shapes=[pltpu.VMEM((n_slab, m, LANES), F32),
                        pltpu.VMEM((n_slab, m, LANES), F32),
                        pltpu.VMEM((m, 2 * n_state), F32),
                        pltpu.VMEM((nb, 2 * n_state), F32)],
        compiler_params=pltpu.CompilerParams(
            dimension_semantics=("arbitrary",), vmem_limit_bytes=VMEM_LIMIT),
        name="s5",
    )(h3, h0, lam_b, bmat, cmat, d_slab)


def _lo_lanes():
    return lax.broadcasted_iota(jnp.int32, (1, LANES), 1) < HEAD_DIM


def _head_norm(t, gain2_scaled):
    lo = _lo_lanes()
    sq = t * t
    ss_lo = jnp.sum(jnp.where(lo, sq, 0.0), axis=-1, keepdims=True)
    ss_hi = jnp.sum(jnp.where(lo, 0.0, sq), axis=-1, keepdims=True)
    return t * lax.rsqrt(jnp.where(lo, ss_lo, ss_hi) + HEAD_DIM * EPS) * gain2_scaled


def _pair_blockdiag(t):
    lo = _lo_lanes()
    r = pltpu.roll(t, HEAD_DIM, 1)
    return ((jnp.where(lo, t, 0.0), jnp.where(lo, 0.0, r)),
            (jnp.where(lo, r, 0.0), jnp.where(lo, 0.0, t)))


def _rowmax(parts):
    full, out = None, None
    for p in parts:
        w = p.shape[1]
        for c0 in range(0, w - w % LANES, LANES):
            blk = p[:, c0:c0 + LANES]
            full = blk if full is None else jnp.maximum(full, blk)
        if w % LANES:
            r = jnp.max(p[:, w - w % LANES:], axis=-1, keepdims=True)
            out = r if out is None else jnp.maximum(out, r)
    if full is not None:
        r = jnp.max(full, axis=-1, keepdims=True)
        out = r if out is None else jnp.maximum(out, r)
    return out


def _softmax_numerators(sc, sinks):
    s = sc.shape[1] // 2
    b = (s // LANES) * LANES
    left, mid, right = sc[:, :b], sc[:, b:b + LANES], sc[:, b + LANES:]
    in0 = lax.broadcasted_iota(jnp.int32, (1, LANES), 1) < (s - b)
    m0 = _rowmax([left, jnp.where(in0, mid, sinks[0])])
    m1 = _rowmax([jnp.where(in0, sinks[1], mid), right])
    e = [jnp.exp2(left[:, c:c + LANES] - m0) for c in range(0, b, LANES)]
    e.append(jnp.exp2(mid - jnp.where(in0, m0, m1)))
    e += [jnp.exp2(right[:, c:c + LANES] - m1) for c in range(0, right.shape[1], LANES)]
    sink_share = jnp.exp2(sinks[2] - jnp.where(_lo_lanes(), m0, m1))
    return jnp.concatenate(e, axis=1).astype(BF16), sink_share


def _scores(qn, kbd, bias):
    return lax.dot_general(qn, kbd, (((1,), (1,)), ((), ())), preferred_element_type=F32) + bias


def _weighted_values(e, vbd1, sink_share):
    od = jnp.dot(e, vbd1, preferred_element_type=F32)
    return od[:, :LANES] / (od[:, LANES:] + sink_share)


def _band_attn_body(q_ref, k_ref, v_ref, qg_ref, kg_ref, bias_ref, sink_ref, ones_ref, o_ref, pk_ref, pv_ref,
                    kvbd_ref, qn_ref, sc0_ref, sc1_ref, e0_ref, e1_ref, share0_ref, share1_ref, *, seq, qb):
    j = pl.program_id(1)
    pad = BAND_CHUNKS * CHUNK
    span = pad + CHUNK
    rows = min(512, seq)
    n_pair = GQA_GROUP // 2
    n_chunk = qb // CHUNK
    sc_refs, e_refs, share_refs = (sc0_ref, sc1_ref), (e0_ref, e1_ref), (share0_ref, share1_ref)
    assert n_chunk >= 4 and n_chunk % 2 == 0

    @pl.when(j == 0)
    def _():
        zeros = jnp.zeros((pad, LANES), BF16)
        for a in range(8):
            kvbd_ref[a, 0:pad, :] = zeros

        def fill(i, carry):
            r0 = pl.multiple_of(i * rows, rows)
            kn = _head_norm(k_ref[0, pl.ds(r0, rows), :].astype(F32), kg_ref[...])
            for is_v, t in enumerate((kn, v_ref[0, pl.ds(r0, rows), :].astype(F32))):
                for kv, halves in enumerate(_pair_blockdiag(t)):
                    for tb, half in enumerate(halves):
                        kvbd_ref[4 * is_v + 2 * kv + tb, pl.ds(pad + r0, rows), :] = half.astype(BF16)
            return carry

        lax.fori_loop(0, seq // rows, fill, 0)
        pk_ref[0] = _head_norm(k_ref[0, seq - WINDOW:seq, :].astype(F32), kg_ref[...])
        pv_ref[0] = v_ref[0, seq - WINDOW:seq, :].astype(F32)

    def lanes_of(kv, i):
        return slice((kv * n_pair + i) * LANES, (kv * n_pair + i + 1) * LANES)

    def stage_norm(c):
        q0 = pl.multiple_of(c * CHUNK, CHUNK)
        for kv in range(2):
            for i in range(n_pair):
                qn_ref[c, kv, i * CHUNK:(i + 1) * CHUNK, :] = _head_norm(
                    q_ref[0, pl.ds(q0, CHUNK), lanes_of(kv, i)].astype(F32), qg_ref[...]).astype(BF16)

    def key_rows(c):
        return pl.ds(pl.multiple_of((j * n_chunk + c) * CHUNK, CHUNK), span)

    def stage_scores(c, slot):
        variant = jnp.minimum(j * n_chunk + c, BAND_CHUNKS)
        for kv in range(2):
            kbd = jnp.concatenate([kvbd_ref[2 * kv, key_rows(c), :], kvbd_ref[2 * kv + 1, key_rows(c), :]], axis=0)
            sc_refs[slot][kv] = _scores(qn_ref[c, kv], kbd, bias_ref[variant, kv])

    def stage_exp(slot):
        for kv in range(2):
            e_refs[slot][kv], share_refs[slot][kv] = _softmax_numerators(sc_refs[slot][kv], sink_ref[kv])

    def stage_values(c, slot):
        q0 = pl.multiple_of(c * CHUNK, CHUNK)
        for kv in range(2):
            vbd = jnp.concatenate([kvbd_ref[4 + 2 * kv, key_rows(c), :], kvbd_ref[4 + 2 * kv + 1, key_rows(c), :]], axis=0)
            o = _weighted_values(e_refs[slot][kv], jnp.concatenate([vbd, ones_ref[...]], axis=1), share_refs[slot][kv])
            for i in range(n_pair):
                o_ref[0, pl.ds(q0, CHUNK), lanes_of(kv, i)] = o[i * CHUNK:(i + 1) * CHUNK].astype(BF16)

    def iteration(i, parity):
        live = (lambda c: 0 <= c < n_chunk) if isinstance(i, int) else (lambda c: True)
        if live(i - 3):
            stage_values(i - 3, 1 - parity)
        if live(i - 2):
            stage_exp(parity)
        if live(i - 1):
            stage_scores(i - 1, 1 - parity)
        if live(i):
            stage_norm(i)

    for i in range(4):
        iteration(i, i % 2)

    def steady(p, carry):
        iteration(4 + 2 * p, 0)
        iteration(5 + 2 * p, 1)
        return carry

    lax.fori_loop(0, (n_chunk - 4) // 2, steady, 0)
    for i in range(n_chunk, n_chunk + 3):
        iteration(i, i % 2)


def _band_attn(h3, qcol, kcol, vcol, qgain2, kgain2, bias, sinkcol, ones_bd, qb):
    nb, seq, _ = h3.shape
    aw = 2 * GQA_GROUP * HEAD_DIM
    pad = BAND_CHUNKS * CHUNK
    body = functools.partial(_band_attn_body, seq=seq, qb=qb)
    const = lambda *shape: pl.BlockSpec(shape, lambda b, j: (0,) * len(shape))
    return pl.pallas_call(
        body,
        grid=(nb, seq // qb),
        in_specs=[pl.BlockSpec((1, qb, aw), lambda b, j: (b, j, qcol // aw)),
                  pl.BlockSpec((1, seq, LANES), lambda b, j: (b, 0, kcol // LANES)),
                  pl.BlockSpec((1, seq, LANES), lambda b, j: (b, 0, vcol // LANES)),
                  const(1, LANES), const(1, LANES),
                  const(*bias.shape), const(*sinkcol.shape), const(*ones_bd.shape)],
        out_specs=[pl.BlockSpec((1, qb, aw), lambda b, j: (b, j, 0)),
                   pl.BlockSpec((1, WINDOW, LANES), lambda b, j: (b, 0, 0)),
                   pl.BlockSpec((1, WINDOW, LANES), lambda b, j: (b, 0, 0))],
        out_shape=[jax.ShapeDtypeStruct((nb, seq, aw), BF16),
                   jax.ShapeDtypeStruct((nb, WINDOW, LANES), F32),
                   jax.ShapeDtypeStruct((nb, WINDOW, LANES), F32)],
        scratch_shapes=[pltpu.VMEM((8, seq + pad, LANES), BF16),
                        pltpu.VMEM((qb // CHUNK, 2, (GQA_GROUP // 2) * CHUNK, LANES), BF16),
                        *[pltpu.VMEM((2, (GQA_GROUP // 2) * CHUNK, 2 * (pad + CHUNK)), F32)] * 2,
                        *[pltpu.VMEM((2, (GQA_GROUP // 2) * CHUNK, 2 * (pad + CHUNK)), BF16)] * 2,
                        *[pltpu.VMEM((2, (GQA_GROUP // 2) * CHUNK, LANES), F32)] * 2],
        compiler_params=pltpu.CompilerParams(
            dimension_semantics=("parallel", "arbitrary"), vmem_limit_bytes=VMEM_LIMIT),
        name="band_attn",
    )(h3, h3, h3, qgain2, kgain2, bias, sinkcol, ones_bd)


def _cached_attn_body(q_ref, k_ref, v_ref, ck_ref, cv_ref, qg_ref, kg_ref, bias_ref, sink_ref, ones_ref,
                      o_ref, sk_ref, sv_ref):
    t = q_ref.shape[1]
    n_pair = GQA_GROUP // 2
    kn = _head_norm(k_ref[0].astype(F32), kg_ref[...])
    vn = v_ref[0].astype(F32)
    sk_ref[0] = kn
    sv_ref[0] = vn
    kbds = _pair_blockdiag(jnp.concatenate([ck_ref[0], kn], axis=0))
    vbds = _pair_blockdiag(jnp.concatenate([cv_ref[0], vn], axis=0))
    for kv in range(2):
        qn = jnp.concatenate(
            [_head_norm(q_ref[0, :, (kv * n_pair + i) * LANES:(kv * n_pair + i + 1) * LANES].astype(F32), qg_ref[...])
             for i in range(n_pair)], axis=0).astype(BF16)
        kbd = jnp.concatenate(kbds[kv], axis=0).astype(BF16)
        vbd1 = jnp.concatenate([jnp.concatenate(vbds[kv], axis=0).astype(BF16), ones_ref[...]], axis=1)
        e, sink_share = _softmax_numerators(_scores(qn, kbd, bias_ref[0, kv]), sink_ref[kv])
        o = _weighted_values(e, vbd1, sink_share)
        for i in range(n_pair):
            o_ref[0, :, (kv * n_pair + i) * LANES:(kv * n_pair + i + 1) * LANES] = o[i * t:(i + 1) * t].astype(BF16)


def _cached_attn(h3, qcol, kcol, vcol, cache_k, cache_v, qgain2, kgain2, bias, sinkcol, ones_bd):
    nb, t, _ = h3.shape
    aw = 2 * GQA_GROUP * HEAD_DIM
    r = cache_k.shape[1]
    const = lambda *shape: pl.BlockSpec(shape, lambda b: (0,) * len(shape))
    return pl.pallas_call(
        _cached_attn_body,
        grid=(nb,),
        in_specs=[pl.BlockSpec((1, t, aw), lambda b: (b, 0, qcol // aw)),
                  pl.BlockSpec((1, t, LANES), lambda b: (b, 0, kcol // LANES)),
                  pl.BlockSpec((1, t, LANES), lambda b: (b, 0, vcol // LANES)),
                  pl.BlockSpec((1, r, LANES), lambda b: (b, 0, 0)),
                  pl.BlockSpec((1, r, LANES), lambda b: (b, 0, 0)),
                  const(1, LANES), const(1, LANES),
                  const(*bias.shape), const(*sinkcol.shape), const(*ones_bd.shape)],
        out_specs=[pl.BlockSpec((1, t, aw), lambda b: (b, 0, 0)),
                   pl.BlockSpec((1, t, LANES), lambda b: (b, 0, 0)),
                   pl.BlockSpec((1, t, LANES), lambda b: (b, 0, 0))],
        out_shape=[jax.ShapeDtypeStruct((nb, t, aw), BF16),
                   jax.ShapeDtypeStruct((nb, t, LANES), F32),
                   jax.ShapeDtypeStruct((nb, t, LANES), F32)],
        compiler_params=pltpu.CompilerParams(
            dimension_semantics=("parallel",), vmem_limit_bytes=VMEM_LIMIT),
        name="cached_attn",
    )(h3, h3, h3, cache_k, cache_v, qgain2, kgain2, bias, sinkcol, ones_bd)


def _sigmoid(x):
    return 1.0 / (1.0 + jnp.exp(-x))


def _merge_body(x_ref, y_ref, za_ref, o_ref, zb_ref, ga_ref, gb_ref,
                wglu_ref, bglu_ref, woa_ref, wob_ref, wo_ref, out_ref, *, n_part):
    part = x_ref.shape[0] // n_part
    for p in range(n_part):
        r = slice(p * part, (p + 1) * part)
        y = y_ref[r, :].astype(F32)
        g = 0.5 * y * (1.0 + lax.erf(y * (2.0 ** -0.5)))
        gl = jnp.dot(g.astype(BF16), wglu_ref[...], preferred_element_type=F32) + bglu_ref[...]
        za = za_ref[r, :].astype(F32)
        br_a = g * _sigmoid(gl) * (za * _sigmoid(za))
        zb = zb_ref[r, :].astype(F32)
        br_b = o_ref[r, :].astype(F32) * (zb * _sigmoid(zb))
        mixed = (_sigmoid(ga_ref[r, :].astype(F32)) * jnp.dot(br_a.astype(BF16), woa_ref[...], preferred_element_type=F32)
                 + _sigmoid(gb_ref[r, :].astype(F32)) * jnp.dot(br_b.astype(BF16), wob_ref[...], preferred_element_type=F32))
        out_ref[r, :] = x_ref[r, :] + jnp.dot(mixed.astype(BF16), wo_ref[...], preferred_element_type=F32)


def _merge(x2, y2, o2, h2, cols, wglu, bglu, woa, wob, wo, tm, n_part):
    n, d = x2.shape
    sw = y2.shape[1]
    aw = o2.shape[1]
    za_col, zb_col, ga_col, gb_col = cols
    row = lambda width, col: pl.BlockSpec((pl.Element(tm), pl.Element(width)), lambda i: (i * tm, col))
    weight = lambda a: pl.BlockSpec(a.shape, lambda i: (0, 0), pipeline_mode=pl.Buffered(1))
    return pl.pallas_call(
        functools.partial(_merge_body, n_part=n_part),
        grid=(n // tm,),
        in_specs=[row(d, 0), row(sw, 0), row(sw, za_col), row(aw, 0), row(aw, zb_col),
                  row(d, ga_col), row(d, gb_col),
                  weight(wglu), weight(bglu), weight(woa), weight(wob), weight(wo)],
        out_specs=row(d, 0),
        out_shape=jax.ShapeDtypeStruct((n, d), F32),
        compiler_params=pltpu.CompilerParams(
            dimension_semantics=("parallel",), vmem_limit_bytes=VMEM_LIMIT),
        name="merge",
    )(x2, y2, h2, o2, h2, h2, h2, wglu, bglu, woa, wob, wo)


def _discretize(a_re, a_im, log_dt, b_re, b_im):
    dt = jnp.exp(log_dt)[:, None]
    mag = jnp.exp(a_re * dt)
    ang = a_im * dt
    lam_re = mag * jnp.cos(ang)
    lam_im = mag * jnp.sin(ang)
    den = a_re * a_re + a_im * a_im
    cr = ((lam_re - 1.0) * a_re + lam_im * a_im) / den
    ci = (lam_im * a_re - (lam_re - 1.0) * a_im) / den
    bb_re = cr[..., None] * b_re - ci[..., None] * b_im
    bb_im = cr[..., None] * b_im + ci[..., None] * b_re
    return lam_re, lam_im, bb_re, bb_im


def _slab_blockdiag(t):
    n_slab, g, a, b = t.shape
    same_group = jnp.eye(g, dtype=bool)[None, :, None, :, None]
    return jnp.where(same_group, t[:, :, :, None, :], 0.0).reshape(n_slab, g * a, g * b)


def _t5_bucket(rel):
    half = N_BUCKETS // 2
    n = -rel
    ret = jnp.where(n < 0, half, 0)
    n = jnp.abs(n)
    max_exact = half // 2
    nf = jnp.maximum(n, 1).astype(F32)
    large = max_exact + (jnp.log(nf / max_exact) / math.log(MAX_DISTANCE / max_exact)
                         * (half - max_exact)).astype(jnp.int32)
    large = jnp.minimum(large, half - 1)
    return ret + jnp.where(n < max_exact, n, large)


def _pair_bias(rel, table):
    t, s = rel.shape
    onehot = (_t5_bucket(rel)[..., None] == jnp.arange(N_BUCKETS)).astype(F32)
    b = jnp.einsum("tsn,nh->tsh", onehot, table.astype(F32) * LOG2E, precision=lax.Precision.HIGHEST)
    b = b.reshape(t, s, 2, GQA_GROUP // 2, 2)
    return jnp.transpose(b, (2, 3, 0, 4, 1)).reshape(2, (GQA_GROUP // 2) * t, 2 * s)


def _pair_sinks(sinks, t):
    sk = jnp.transpose(sinks.astype(F32).reshape(2, GQA_GROUP // 2, 2), (0, 2, 1))[:, :, :, None, None]
    sk = jnp.broadcast_to(sk, (2, 2, GQA_GROUP // 2, t, LANES)).reshape(2, 2, (GQA_GROUP // 2) * t, LANES)
    lo = jnp.arange(LANES) < HEAD_DIM
    return jnp.concatenate([sk, jnp.where(lo, sk[:, 0], sk[:, 1])[:, None]], axis=1) * LOG2E


def _pair_ones(s):
    first = (jnp.arange(2 * s) < s)[:, None]
    lo = (jnp.arange(LANES) < HEAD_DIM)[None, :]
    return (first == lo).astype(BF16)


def kernel(x_prompt, x_sample, cache_k, cache_v, state_ssm_re, state_ssm_im, norm_gain, w_in, ssm_a_re, ssm_a_im, ssm_log_dt, ssm_b_re, ssm_b_im, ssm_c_re, ssm_c_im, ssm_d, w_glu, b_glu, q_gain, k_gain, attn_sinks, rel_bias, w_out_a, w_out_b, w_o):
    nb, seq, d = x_prompt.shape
    db, dseq, _ = x_sample.shape
    n_groups, n_st = ssm_a_re.shape[1:]
    sw = n_groups * SSM_GROUP
    aw = 2 * GQA_GROUP * HEAD_DIM
    kvw = 2 * HEAD_DIM
    n_slab = sw // LANES
    n_state = n_groups * n_st
    l = 0

    c_u, c_za, c_q = 0, sw, 2 * sw
    c_k = c_q + aw
    c_v = c_k + kvw
    c_zb = c_v + kvw
    c_ga = c_zb + aw
    c_gb = c_ga + d
    w_bf = w_in[l].astype(BF16)
    gain = norm_gain[l].astype(F32).reshape(1, d)

    lam_re, lam_im, bb_re, bb_im = _discretize(
        ssm_a_re[l].astype(F32), ssm_a_im[l].astype(F32), ssm_log_dt[l].astype(F32),
        ssm_b_re[l].astype(F32), ssm_b_im[l].astype(F32))
    lam_b = jnp.broadcast_to(jnp.stack([lam_re.reshape(-1), lam_im.reshape(-1)])[:, None, :], (2, nb, n_state))
    slab = lambda t: t.reshape(n_slab, SLAB_GROUPS, *t.shape[1:])
    bmat = jnp.concatenate([_slab_blockdiag(jnp.swapaxes(slab(bb_re), 2, 3)),
                            _slab_blockdiag(jnp.swapaxes(slab(bb_im), 2, 3))], axis=2).astype(BF16)
    cmat = jnp.concatenate([_slab_blockdiag(jnp.swapaxes(slab(ssm_c_re[l].astype(F32)), 2, 3)),
                            _slab_blockdiag(jnp.swapaxes(slab(-ssm_c_im[l].astype(F32)), 2, 3))], axis=1).astype(BF16)
    d_slab = ssm_d[l].astype(F32).reshape(n_slab, 1, LANES)

    qgain2 = jnp.tile(q_gain[l].astype(F32) * (HEAD_DIM ** 0.5 * SCALE * LOG2E), 2).reshape(1, LANES)
    kgain2 = jnp.tile(k_gain[l].astype(F32) * HEAD_DIM ** 0.5, 2).reshape(1, LANES)
    span = (BAND_CHUNKS + 1) * CHUNK
    rel_p = jnp.arange(span)[None, :] - BAND_CHUNKS * CHUNK - jnp.arange(CHUNK)[:, None]
    bias_p = _pair_bias(rel_p, rel_bias)
    key_off = jnp.tile(jnp.arange(span), 2)[None, None, None, :]
    first_valid = ((BAND_CHUNKS - jnp.arange(BAND_CHUNKS + 1)) * CHUNK)[:, None, None, None]
    bias_p = jnp.where(key_off >= first_valid, bias_p[None], -jnp.inf)
    sink_p = _pair_sinks(attn_sinks[l], CHUNK)
    rows = cache_k.shape[2]
    rel_s = jnp.arange(rows + dseq)[None, :] - rows - jnp.arange(dseq)[:, None]
    bias_s = _pair_bias(rel_s, rel_bias)[None]
    sink_s = _pair_sinks(attn_sinks[l], dseq)

    wglu = w_glu[l].astype(BF16)
    bglu = b_glu[l].astype(F32).reshape(1, sw)
    woa = w_out_a[l].astype(BF16)
    wob = w_out_b[l].astype(BF16)
    wo = w_o[l].astype(BF16)
    merge_cols = (c_za, c_zb, c_ga, c_gb)

    xp2 = x_prompt.reshape(nb * seq, d)
    hp = _inproj(xp2, gain, w_bf, tm=min(256, nb * seq), n_split=3)
    hp3 = hp.reshape(nb, seq, -1)
    yp, hfin_p = _s5(hp3, jnp.zeros((nb, 2 * n_state), F32), lam_b, bmat, cmat, d_slab, tl=min(64, seq), n_sub=2)
    op, pk, pv = _band_attn(hp3, c_q, c_k, c_v, qgain2, kgain2, bias_p, sink_p, _pair_ones(span), qb=min(2048, seq))
    y_p = _merge(xp2, yp.reshape(nb * seq, sw), op.reshape(nb * seq, aw), hp, merge_cols,
                 wglu, bglu, woa, wob, wo, tm=512, n_part=2).reshape(nb, seq, d)

    xs2 = x_sample.reshape(db * dseq, d)
    hs = _inproj(xs2, gain, w_bf, tm=db * dseq, n_split=3)
    hs3 = hs.reshape(db, dseq, -1)
    h0 = jnp.concatenate([state_ssm_re[l].reshape(db, n_state), state_ssm_im[l].reshape(db, n_state)],
                         axis=1).astype(F32)
    ys, hfin_s = _s5(hs3, h0, lam_b, bmat, cmat, d_slab, tl=dseq, n_sub=1)
    ck = cache_k[l].reshape(db, rows, kvw).astype(F32)
    cv = cache_v[l].reshape(db, rows, kvw).astype(F32)
    os_, sk, sv = _cached_attn(hs3, c_q, c_k, c_v, ck, cv, qgain2, kgain2, bias_s, sink_s, _pair_ones(rows + dseq))
    y_s = _merge(xs2, ys.reshape(db * dseq, sw), os_.reshape(db * dseq, aw), hs, merge_cols,
                 wglu, bglu, woa, wob, wo, tm=db * dseq, n_part=1).reshape(db, dseq, d)

    st = lambda h, nbb: (h[:, :n_state].reshape(1, nbb, n_groups, n_st), h[:, n_state:].reshape(1, nbb, n_groups, n_st))
    p_re, p_im = st(hfin_p, nb)
    s_re, s_im = st(hfin_s, db)
    kvshape = lambda a: a.reshape(1, a.shape[0], a.shape[1], 2, HEAD_DIM)
    return (y_p, y_s, p_re, p_im, kvshape(pk), kvshape(pv), s_re, s_im, kvshape(sk), kvshape(sv))
```

```python
import functools
import math

import jax
import jax.numpy as jnp
from jax import lax
from jax.experimental import pallas as pl
from jax.experimental.pallas import tpu as pltpu

F32 = jnp.float32
BF16 = jnp.bfloat16

LANES = 128
SUBLANES = 8
VMEM_LIMIT = 56 * 1024 * 1024

CHUNK = 64
HEAD_DIM = 64
GQA_GROUP = 8
SSM_GROUP = 16
SSM_STATE = 64
WINDOW = 128
BAND_CHUNKS = 2
N_BUCKETS = 32
MAX_DISTANCE = 128
EPS = 1e-6
SCALE = HEAD_DIM ** -0.5
LOG2E = math.log2(math.e)
SLAB_GROUPS = LANES // SSM_GROUP
SLAB_STATES = SLAB_GROUPS * SSM_STATE
SCAN_LANES = 1024


def _inproj_body(xp_ref, xs_ref, g_ref, w_hbm, hp_ref, hs_ref, w_ref, stage_ref, sem, *, n_tile, cast_rows):
    i = pl.program_id(0)
    d = w_ref.shape[0]
    n_chunk = d // cast_rows

    def chunk_copy(c, slot):
        return pltpu.make_async_copy(w_hbm.at[pl.ds(c * cast_rows, cast_rows), :], stage_ref.at[slot], sem.at[slot])

    @pl.when(i == 0)
    def _():
        chunk_copy(0, 0).start()

        def cast(c, carry):
            slot = c % 2

            @pl.when(c + 1 < n_chunk)
            def _():
                chunk_copy(c + 1, 1 - slot).start()

            chunk_copy(c, slot).wait()
            w_ref[pl.ds(pl.multiple_of(c * cast_rows, cast_rows), cast_rows), :] = stage_ref[slot].astype(BF16)
            return carry

        lax.fori_loop(0, n_chunk, cast, 0)

    def project(x_ref, o_ref):
        x = x_ref[...]
        ms = jnp.mean(x * x, axis=-1, keepdims=True)
        xn = (x * lax.rsqrt(ms + EPS) * g_ref[...]).astype(BF16)
        o_ref[...] = jnp.dot(xn, w_ref[...], preferred_element_type=F32).astype(BF16)

    @pl.when(i < n_tile)
    def _():
        project(xp_ref, hp_ref)

    @pl.when(i == n_tile)
    def _():
        project(xs_ref, hs_ref)


def _inproj(xp2, xs2, gain, w_f32, tm, cast_rows):
    n, d = xp2.shape
    ns = xs2.shape[0]
    width = w_f32.shape[1]
    n_tile = n // tm
    last = lambda i: (jnp.minimum(i, n_tile - 1), 0)
    return pl.pallas_call(
        functools.partial(_inproj_body, n_tile=n_tile, cast_rows=cast_rows),
        grid=(n_tile + 1,),
        in_specs=[pl.BlockSpec((tm, d), last),
                  pl.BlockSpec((ns, d), lambda i: (0, 0)),
                  pl.BlockSpec((1, d), lambda i: (0, 0)),
                  pl.BlockSpec(memory_space=pl.ANY)],
        out_specs=[pl.BlockSpec((tm, width), last),
                   pl.BlockSpec((ns, width), lambda i: (0, 0))],
        out_shape=[jax.ShapeDtypeStruct((n, width), BF16),
                   jax.ShapeDtypeStruct((ns, width), BF16)],
        scratch_shapes=[pltpu.VMEM((d, width), BF16),
                        pltpu.VMEM((2, cast_rows, width), F32),
                        pltpu.SemaphoreType.DMA((2,))],
        compiler_params=pltpu.CompilerParams(
            dimension_semantics=("arbitrary",), vmem_limit_bytes=VMEM_LIMIT),
        name="inproj",
    )(xp2, xs2, gain, w_f32)


def _s5_body(u_ref, h0_ref, lam_ref, bmat_ref, cmat_ref, d_ref, y_ref, hout_ref,
             utb_ref, ytb_ref, bu_ref, hst_ref, *, nb, tl, n_slab, n_state, n_sub):
    sub_t = tl // n_sub
    sub_m = sub_t * nb
    n_pass = n_state // SCAN_LANES

    @pl.when(pl.program_id(0) == 0)
    def _():
        hst_ref[...] = h0_ref[...]

    for s in range(n_slab):
        for b in range(nb):
            utb_ref[s, pl.ds(b, tl, stride=nb), :] = u_ref[b, :, s * LANES:(s + 1) * LANES].astype(F32)

    def re_cols(s, width=SLAB_STATES):
        return slice(s * width, (s + 1) * width)

    def im_cols(s, width=SLAB_STATES):
        return slice(n_state + s * width, n_state + (s + 1) * width)

    def project_in(k):
        rows = slice(k * sub_m, (k + 1) * sub_m)
        for s in range(n_slab):
            bu = jnp.dot(utb_ref[s, rows, :].astype(BF16), bmat_ref[s], preferred_element_type=F32)
            bu_ref[rows, re_cols(s)] = bu[:, :SLAB_STATES]
            bu_ref[rows, im_cols(s)] = bu[:, SLAB_STATES:]

    def scan(k, state):
        out = []
        for c, (hr, hi) in enumerate(state):
            re, im = re_cols(c, SCAN_LANES), im_cols(c, SCAN_LANES)
            lr = lam_ref[0, :, re]
            li = lam_ref[1, :, re]
            for t in range(k * sub_t, (k + 1) * sub_t):
                r = slice(t * nb, (t + 1) * nb)
                hr, hi = lr * hr - li * hi + bu_ref[r, re], lr * hi + li * hr + bu_ref[r, im]
                bu_ref[r, re] = hr
                bu_ref[r, im] = hi
            out.append((hr, hi))
        return out

    def project_out(k):
        rows = slice(k * sub_m, (k + 1) * sub_m)
        for s in range(n_slab):
            y = (jnp.dot(bu_ref[rows, re_cols(s)].astype(BF16), cmat_ref[s, :SLAB_STATES, :], preferred_element_type=F32)
                 + jnp.dot(bu_ref[rows, im_cols(s)].astype(BF16), cmat_ref[s, SLAB_STATES:, :], preferred_element_type=F32))
            ytb_ref[s, rows, :] = y + d_ref[s] * utb_ref[s, rows, :]

    state = [(hst_ref[:, re_cols(c, SCAN_LANES)], hst_ref[:, im_cols(c, SCAN_LANES)]) for c in range(n_pass)]
    project_in(0)
    for k in range(n_sub):
        if k + 1 < n_sub:
            project_in(k + 1)
        state = scan(k, state)
        if k >= 1:
            project_out(k - 1)
    project_out(n_sub - 1)
    for c, (hr, hi) in enumerate(state):
        hst_ref[:, re_cols(c, SCAN_LANES)] = hr
        hst_ref[:, im_cols(c, SCAN_LANES)] = hi
    hout_ref[...] = hst_ref[...]

    for s in range(n_slab):
        for b in range(nb):
            y_ref[b, :, s * LANES:(s + 1) * LANES] = ytb_ref[s, pl.ds(b, tl, stride=nb), :].astype(BF16)


def _s5(h3, h0, lam_b, bmat, cmat, d_slab, tl, n_sub):
    nb, seq, _ = h3.shape
    n_slab = bmat.shape[0]
    width = n_slab * LANES
    n_state = n_slab * SLAB_STATES
    m = tl * nb
    body = functools.partial(_s5_body, nb=nb, tl=tl, n_slab=n_slab, n_state=n_state, n_sub=n_sub)
    const = lambda *shape: pl.BlockSpec(shape, lambda i: (0,) * len(shape))
    return pl.pallas_call(
        body,
        grid=(seq // tl,),
        in_specs=[pl.BlockSpec((nb, tl, width), lambda i: (0, i, 0)),
                  const(nb, 2 * n_state),
                  const(2, nb, n_state),
                  const(n_slab, LANES, 2 * SLAB_STATES),
                  const(n_slab, 2 * SLAB_STATES, LANES),
                  const(n_slab, 1, LANES)],
        out_specs=[pl.BlockSpec((nb, tl, width), lambda i: (0, i, 0)),
                   const(nb, 2 * n_state)],
        out_shape=[jax.ShapeDtypeStruct((nb, seq, width), BF16),
                   jax.ShapeDtypeStruct((nb, 2 * n_state), F32)],
        scratch_shapes=[pltpu.VMEM((n_slab, m, LANES), F32),
                        pltpu.VMEM((n_slab, m, LANES), F32),
                        pltpu.VMEM((m, 2 * n_state), F32),
                        pltpu.VMEM((nb, 2 * n_state), F32)],
        compiler_params=pltpu.CompilerParams(
            dimension_semantics=("arbitrary",), vmem_limit_bytes=VMEM_LIMIT),
        name="s5",
    )(h3, h0, lam_b, bmat, cmat, d_slab)


def _lo_lanes():
    return lax.broadcasted_iota(jnp.int32, (1, LANES), 1) < HEAD_DIM


def _head_norm(t, gain2_scaled):
    lo = _lo_lanes()
    sq = t * t
    ss_lo = jnp.sum(jnp.where(lo, sq, 0.0), axis=-1, keepdims=True)
    ss_hi = jnp.sum(jnp.where(lo, 0.0, sq), axis=-1, keepdims=True)
    return t * lax.rsqrt(jnp.where(lo, ss_lo, ss_hi) + HEAD_DIM * EPS) * gain2_scaled


def _pair_blockdiag(t):
    lo = _lo_lanes()
    r = pltpu.roll(t, HEAD_DIM, 1)
    return ((jnp.where(lo, t, 0.0), jnp.where(lo, 0.0, r)),
            (jnp.where(lo, r, 0.0), jnp.where(lo, 0.0, t)))


def _rowmax(parts):
    full, out = None, None
    for p in parts:
        w = p.shape[1]
        for c0 in range(0, w - w % LANES, LANES):
            blk = p[:, c0:c0 + LANES]
            full = blk if full is None else jnp.maximum(full, blk)
        if w % LANES:
            r = jnp.max(p[:, w - w % LANES:], axis=-1, keepdims=True)
            out = r if out is None else jnp.maximum(out, r)
    if full is not None:
        r = jnp.max(full, axis=-1, keepdims=True)
        out = r if out is None else jnp.maximum(out, r)
    return out


def _softmax_numerators(sc, sinks):
    s = sc.shape[1] // 2
    b = (s // LANES) * LANES
    left, mid, right = sc[:, :b], sc[:, b:b + LANES], sc[:, b + LANES:]
    in0 = lax.broadcasted_iota(jnp.int32, (1, LANES), 1) < (s - b)
    m0 = _rowmax([left, jnp.where(in0, mid, sinks[0])])
    m1 = _rowmax([jnp.where(in0, sinks[1], mid), right])
    e = [jnp.exp2(left[:, c:c + LANES] - m0) for c in range(0, b, LANES)]
    e.append(jnp.exp2(mid - jnp.where(in0, m0, m1)))
    e += [jnp.exp2(right[:, c:c + LANES] - m1) for c in range(0, right.shape[1], LANES)]
    sink_share = jnp.exp2(sinks[2] - jnp.where(_lo_lanes(), m0, m1))
    return jnp.concatenate(e, axis=1).astype(BF16), sink_share


def _scores(qn, kbd, bias):
    return lax.dot_general(qn, kbd, (((1,), (1,)), ((), ())), preferred_element_type=F32) + bias


def _weighted_values(e, vbd1, sink_share):
    od = jnp.dot(e, vbd1, preferred_element_type=F32)
    return od[:, :LANES] / (od[:, LANES:] + sink_share)


def _band_attn_body(q_ref, k_ref, v_ref, qg_ref, kg_ref, bias_ref, sink_ref, ones_ref, o_ref, pk_ref, pv_ref,
                    kvbd_ref, qn_ref, sc0_ref, sc1_ref, e0_ref, e1_ref, share0_ref, share1_ref, *, seq, qb):
    j = pl.program_id(1)
    pad = BAND_CHUNKS * CHUNK
    span = pad + CHUNK
    rows = min(512, seq)
    n_pair = GQA_GROUP // 2
    n_chunk = qb // CHUNK
    sc_refs, e_refs, share_refs = (sc0_ref, sc1_ref), (e0_ref, e1_ref), (share0_ref, share1_ref)
    assert n_chunk >= 4 and n_chunk % 2 == 0

    @pl.when(j == 0)
    def _():
        zeros = jnp.zeros((pad, LANES), BF16)
        for a in range(8):
            kvbd_ref[a, 0:pad, :] = zeros

        def fill(i, carry):
            r0 = pl.multiple_of(i * rows, rows)
            kn = _head_norm(k_ref[0, pl.ds(r0, rows), :].astype(F32), kg_ref[...])
            for is_v, t in enumerate((kn, v_ref[0, pl.ds(r0, rows), :].astype(F32))):
                for kv, halves in enumerate(_pair_blockdiag(t)):
                    for tb, half in enumerate(halves):
                        kvbd_ref[4 * is_v + 2 * kv + tb, pl.ds(pad + r0, rows), :] = half.astype(BF16)
            return carry

        lax.fori_loop(0, seq // rows, fill, 0)
        pk_ref[0] = _head_norm(k_ref[0, seq - WINDOW:seq, :].astype(F32), kg_ref[...])
        pv_ref[0] = v_ref[0, seq - WINDOW:seq, :].astype(F32)

    def lanes_of(kv, i):
        return slice((kv * n_pair + i) * LANES, (kv * n_pair + i + 1) * LANES)

    def stage_norm(c):
        q0 = pl.multiple_of(c * CHUNK, CHUNK)
        for kv in range(2):
            for i in range(n_pair):
                qn_ref[c, kv, i * CHUNK:(i + 1) * CHUNK, :] = _head_norm(
                    q_ref[0, pl.ds(q0, CHUNK), lanes_of(kv, i)].astype(F32), qg_ref[...]).astype(BF16)

    def key_rows(c):
        return pl.ds(pl.multiple_of((j * n_chunk + c) * CHUNK, CHUNK), span)

    def stage_scores(c, slot):
        variant = jnp.minimum(j * n_chunk + c, BAND_CHUNKS)
        for kv in range(2):
            kbd = jnp.concatenate([kvbd_ref[2 * kv, key_rows(c), :], kvbd_ref[2 * kv + 1, key_rows(c), :]], axis=0)
            sc_refs[slot][kv] = _scores(qn_ref[c, kv], kbd, bias_ref[variant, kv])

    def stage_exp(slot):
        for kv in range(2):
            e_refs[slot][kv], share_refs[slot][kv] = _softmax_numerators(sc_refs[slot][kv], sink_ref[kv])

    def stage_values(c, slot):
        q0 = pl.multiple_of(c * CHUNK, CHUNK)
        for kv in range(2):
            vbd = jnp.concatenate([kvbd_ref[4 + 2 * kv, key_rows(c), :], kvbd_ref[4 + 2 * kv + 1, key_rows(c), :]], axis=0)
            o = _weighted_values(e_refs[slot][kv], jnp.concatenate([vbd, ones_ref[...]], axis=1), share_refs[slot][kv])
            for i in range(n_pair):
                o_ref[0, pl.ds(q0, CHUNK), lanes_of(kv, i)] = o[i * CHUNK:(i + 1) * CHUNK].astype(BF16)

    def iteration(i, parity):
        live = (lambda c: 0 <= c < n_chunk) if isinstance(i, int) else (lambda c: True)
        if live(i - 3):
            stage_values(i - 3, 1 - parity)
        if live(i - 2):
            stage_exp(parity)
        if live(i - 1):
            stage_scores(i - 1, 1 - parity)
        if live(i):
            stage_norm(i)

    for i in range(4):
        iteration(i, i % 2)

    def steady(p, carry):
        iteration(4 + 2 * p, 0)
        iteration(5 + 2 * p, 1)
        return carry

    lax.fori_loop(0, (n_chunk - 4) // 2, steady, 0)
    for i in range(n_chunk, n_chunk + 3):
        iteration(i, i % 2)


def _band_attn(h3, qcol, kcol, vcol, qgain2, kgain2, bias, sinkcol, ones_bd, qb):
    nb, seq, _ = h3.shape
    aw = 2 * GQA_GROUP * HEAD_DIM
    pad = BAND_CHUNKS * CHUNK
    body = functools.partial(_band_attn_body, seq=seq, qb=qb)
    const = lambda *shape: pl.BlockSpec(shape, lambda b, j: (0,) * len(shape))
    return pl.pallas_call(
        body,
        grid=(nb, seq // qb),
        in_specs=[pl.BlockSpec((1, qb, aw), lambda b, j: (b, j, qcol // aw)),
                  pl.BlockSpec((1, seq, LANES), lambda b, j: (b, 0, kcol // LANES)),
                  pl.BlockSpec((1, seq, LANES), lambda b, j: (b, 0, vcol // LANES)),
                  const(1, LANES), const(1, LANES),
                  const(*bias.shape), const(*sinkcol.shape), const(*ones_bd.shape)],
        out_specs=[pl.BlockSpec((1, qb, aw), lambda b, j: (b, j, 0)),
                   pl.BlockSpec((1, WINDOW, LANES), lambda b, j: (b, 0, 0)),
                   pl.BlockSpec((1, WINDOW, LANES), lambda b, j: (b, 0, 0))],
        out_shape=[jax.ShapeDtypeStruct((nb, seq, aw), BF16),
                   jax.ShapeDtypeStruct((nb, WINDOW, LANES), F32),
                   jax.ShapeDtypeStruct((nb, WINDOW, LANES), F32)],
        scratch_shapes=[pltpu.VMEM((8, seq + pad, LANES), BF16),
                        pltpu.VMEM((qb // CHUNK, 2, (GQA_GROUP // 2) * CHUNK, LANES), BF16),
                        *[pltpu.VMEM((2, (GQA_GROUP // 2) * CHUNK, 2 * (pad + CHUNK)), F32)] * 2,
                        *[pltpu.VMEM((2, (GQA_GROUP // 2) * CHUNK, 2 * (pad + CHUNK)), BF16)] * 2,
                        *[pltpu.VMEM((2, (GQA_GROUP // 2) * CHUNK, LANES), F32)] * 2],
        compiler_params=pltpu.CompilerParams(
            dimension_semantics=("parallel", "arbitrary"), vmem_limit_bytes=VMEM_LIMIT),
        name="band_attn",
    )(h3, h3, h3, qgain2, kgain2, bias, sinkcol, ones_bd)


def _cached_attn_body(q_ref, k_ref, v_ref, ck_ref, cv_ref, qg_ref, kg_ref, bias_ref, sink_ref, ones_ref,
                      o_ref, sk_ref, sv_ref):
    t = q_ref.shape[1]
    n_pair = GQA_GROUP // 2
    kn = _head_norm(k_ref[0].astype(F32), kg_ref[...])
    vn = v_ref[0].astype(F32)
    sk_ref[0] = kn
    sv_ref[0] = vn
    kbds = _pair_blockdiag(jnp.concatenate([ck_ref[0], kn], axis=0))
    vbds = _pair_blockdiag(jnp.concatenate([cv_ref[0], vn], axis=0))
    for kv in range(2):
        qn = jnp.concatenate(
            [_head_norm(q_ref[0, :, (kv * n_pair + i) * LANES:(kv * n_pair + i + 1) * LANES].astype(F32), qg_ref[...])
             for i in range(n_pair)], axis=0).astype(BF16)
        kbd = jnp.concatenate(kbds[kv], axis=0).astype(BF16)
        vbd1 = jnp.concatenate([jnp.concatenate(vbds[kv], axis=0).astype(BF16), ones_ref[...]], axis=1)
        e, sink_share = _softmax_numerators(_scores(qn, kbd, bias_ref[0, kv]), sink_ref[kv])
        o = _weighted_values(e, vbd1, sink_share)
        for i in range(n_pair):
            o_ref[0, :, (kv * n_pair + i) * LANES:(kv * n_pair + i + 1) * LANES] = o[i * t:(i + 1) * t].astype(BF16)


def _cached_attn(h3, qcol, kcol, vcol, cache_k, cache_v, qgain2, kgain2, bias, sinkcol, ones_bd):
    nb, t, _ = h3.shape
    aw = 2 * GQA_GROUP * HEAD_DIM
    r = cache_k.shape[1]
    const = lambda *shape: pl.BlockSpec(shape, lambda b: (0,) * len(shape))
    return pl.pallas_call(
        _cached_attn_body,
        grid=(nb,),
        in_specs=[pl.BlockSpec((1, t, aw), lambda b: (b, 0, qcol // aw)),
                  pl.BlockSpec((1, t, LANES), lambda b: (b, 0, kcol // LANES)),
                  pl.BlockSpec((1, t, LANES), lambda b: (b, 0, vcol // LANES)),
                  pl.BlockSpec((1, r, LANES), lambda b: (b, 0, 0)),
                  pl.BlockSpec((1, r, LANES), lambda b: (b, 0, 0)),
                  const(1, LANES), const(1, LANES),
                  const(*bias.shape), const(*sinkcol.shape), const(*ones_bd.shape)],
        out_specs=[pl.BlockSpec((1, t, aw), lambda b: (b, 0, 0)),
                   pl.BlockSpec((1, t, LANES), lambda b: (b, 0, 0)),
                   pl.BlockSpec((1, t, LANES), lambda b: (b, 0, 0))],
        out_shape=[jax.ShapeDtypeStruct((nb, t, aw), BF16),
                   jax.ShapeDtypeStruct((nb, t, LANES), F32),
                   jax.ShapeDtypeStruct((nb, t, LANES), F32)],
        compiler_params=pltpu.CompilerParams(
            dimension_semantics=("parallel",), vmem_limit_bytes=VMEM_LIMIT),
        name="cached_attn",
    )(h3, h3, h3, cache_k, cache_v, qgain2, kgain2, bias, sinkcol, ones_bd)


def _sigmoid(x):
    return 1.0 / (1.0 + jnp.exp(-x))


def _merge_body(x_ref, y_ref, za_ref, o_ref, zb_ref, ga_ref, gb_ref,
                wglu_ref, bglu_ref, woa_ref, wob_ref, wo_ref, out_ref, *, n_part):
    part = x_ref.shape[0] // n_part
    for p in range(n_part):
        r = slice(p * part, (p + 1) * part)
        y = y_ref[r, :].astype(F32)
        g = 0.5 * y * (1.0 + lax.erf(y * (2.0 ** -0.5)))
        gl = jnp.dot(g.astype(BF16), wglu_ref[...], preferred_element_type=F32) + bglu_ref[...]
        za = za_ref[r, :].astype(F32)
        br_a = g * _sigmoid(gl) * (za * _sigmoid(za))
        zb = zb_ref[r, :].astype(F32)
        br_b = o_ref[r, :].astype(F32) * (zb * _sigmoid(zb))
        mixed = (_sigmoid(ga_ref[r, :].astype(F32)) * jnp.dot(br_a.astype(BF16), woa_ref[...], preferred_element_type=F32)
                 + _sigmoid(gb_ref[r, :].astype(F32)) * jnp.dot(br_b.astype(BF16), wob_ref[...], preferred_element_type=F32))
        out_ref[r, :] = x_ref[r, :] + jnp.dot(mixed.astype(BF16), wo_ref[...], preferred_element_type=F32)


def _merge(x2, y2, o2, h2, cols, wglu, bglu, woa, wob, wo, tm, n_part):
    n, d = x2.shape
    sw = y2.shape[1]
    aw = o2.shape[1]
    za_col, zb_col, ga_col, gb_col = cols
    row = lambda width, col: pl.BlockSpec((pl.Element(tm), pl.Element(width)), lambda i: (i * tm, col))
    weight = lambda a: pl.BlockSpec(a.shape, lambda i: (0, 0), pipeline_mode=pl.Buffered(1))
    return pl.pallas_call(
        functools.partial(_merge_body, n_part=n_part),
        grid=(n // tm,),
        in_specs=[row(d, 0), row(sw, 0), row(sw, za_col), row(aw, 0), row(aw, zb_col),
                  row(d, ga_col), row(d, gb_col),
                  weight(wglu), weight(bglu), weight(woa), weight(wob), weight(wo)],
        out_specs=row(d, 0),
        out_shape=jax.ShapeDtypeStruct((n, d), F32),
        compiler_params=pltpu.CompilerParams(
            dimension_semantics=("parallel",), vmem_limit_bytes=VMEM_LIMIT),
        name="merge",
    )(x2, y2, h2, o2, h2, h2, h2, wglu, bglu, woa, wob, wo)


def _discretize(a_re, a_im, log_dt, b_re, b_im):
    dt = jnp.exp(log_dt)[:, None]
    mag = jnp.exp(a_re * dt)
    ang = a_im * dt
    lam_re = mag * jnp.cos(ang)
    lam_im = mag * jnp.sin(ang)
    den = a_re * a_re + a_im * a_im
    cr = ((lam_re - 1.0) * a_re + lam_im * a_im) / den
    ci = (lam_im * a_re - (lam_re - 1.0) * a_im) / den
    bb_re = cr[..., None] * b_re - ci[..., None] * b_im
    bb_im = cr[..., None] * b_im + ci[..., None] * b_re
    return lam_re, lam_im, bb_re, bb_im


def _slab_blockdiag(t):
    n_slab, g, a, b = t.shape
    same_group = jnp.eye(g, dtype=bool)[None, :, None, :, None]
    return jnp.where(same_group, t[:, :, :, None, :], 0.0).reshape(n_slab, g * a, g * b)


def _t5_bucket(rel):
    half = N_BUCKETS // 2
    n = -rel
    ret = jnp.where(n < 0, half, 0)
    n = jnp.abs(n)
    max_exact = half // 2
    nf = jnp.maximum(n, 1).astype(F32)
    large = max_exact + (jnp.log(nf / max_exact) / math.log(MAX_DISTANCE / max_exact)
                         * (half - max_exact)).astype(jnp.int32)
    large = jnp.minimum(large, half - 1)
    return ret + jnp.where(n < max_exact, n, large)


def _pair_bias(rel, table):
    t, s = rel.shape
    onehot = (_t5_bucket(rel)[..., None] == jnp.arange(N_BUCKETS)).astype(F32)
    b = jnp.einsum("tsn,nh->tsh", onehot, table.astype(F32) * LOG2E, precision=lax.Precision.HIGHEST)
    b = b.reshape(t, s, 2, GQA_GROUP // 2, 2)
    return jnp.transpose(b, (2, 3, 0, 4, 1)).reshape(2, (GQA_GROUP // 2) * t, 2 * s)


def _pair_sinks(sinks, t):
    sk = jnp.transpose(sinks.astype(F32).reshape(2, GQA_GROUP // 2, 2), (0, 2, 1))[:, :, :, None, None]
    sk = jnp.broadcast_to(sk, (2, 2, GQA_GROUP // 2, t, LANES)).reshape(2, 2, (GQA_GROUP // 2) * t, LANES)
    lo = jnp.arange(LANES) < HEAD_DIM
    return jnp.concatenate([sk, jnp.where(lo, sk[:, 0], sk[:, 1])[:, None]], axis=1) * LOG2E


def _pair_ones(s):
    first = (jnp.arange(2 * s) < s)[:, None]
    lo = (jnp.arange(LANES) < HEAD_DIM)[None, :]
    return (first == lo).astype(BF16)


def kernel(x_prompt, x_sample, cache_k, cache_v, state_ssm_re, state_ssm_im, norm_gain, w_in, ssm_a_re, ssm_a_im, ssm_log_dt, ssm_b_re, ssm_b_im, ssm_c_re, ssm_c_im, ssm_d, w_glu, b_glu, q_gain, k_gain, attn_sinks, rel_bias, w_out_a, w_out_b, w_o):
    nb, seq, d = x_prompt.shape
    db, dseq, _ = x_sample.shape
    n_groups, n_st = ssm_a_re.shape[1:]
    sw = n_groups * SSM_GROUP
    aw = 2 * GQA_GROUP * HEAD_DIM
    kvw = 2 * HEAD_DIM
    n_slab = sw // LANES
    n_state = n_groups * n_st
    l = 0

    c_u, c_za, c_q = 0, sw, 2 * sw
    c_k = c_q + aw
    c_v = c_k + kvw
    c_zb = c_v + kvw
    c_ga = c_zb + aw
    c_gb = c_ga + d
    gain = norm_gain[l].astype(F32).reshape(1, d)

    lam_re, lam_im, bb_re, bb_im = _discretize(
        ssm_a_re[l].astype(F32), ssm_a_im[l].astype(F32), ssm_log_dt[l].astype(F32),
        ssm_b_re[l].astype(F32), ssm_b_im[l].astype(F32))
    lam_b = jnp.broadcast_to(jnp.stack([lam_re.reshape(-1), lam_im.reshape(-1)])[:, None, :], (2, nb, n_state))
    slab = lambda t: t.reshape(n_slab, SLAB_GROUPS, *t.shape[1:])
    bmat = jnp.concatenate([_slab_blockdiag(jnp.swapaxes(slab(bb_re), 2, 3)),
                            _slab_blockdiag(jnp.swapaxes(slab(bb_im), 2, 3))], axis=2).astype(BF16)
    cmat = jnp.concatenate([_slab_blockdiag(jnp.swapaxes(slab(ssm_c_re[l].astype(F32)), 2, 3)),
                            _slab_blockdiag(jnp.swapaxes(slab(-ssm_c_im[l].astype(F32)), 2, 3))], axis=1).astype(BF16)
    d_slab = ssm_d[l].astype(F32).reshape(n_slab, 1, LANES)

    qgain2 = jnp.tile(q_gain[l].astype(F32) * (HEAD_DIM ** 0.5 * SCALE * LOG2E), 2).reshape(1, LANES)
    kgain2 = jnp.tile(k_gain[l].astype(F32) * HEAD_DIM ** 0.5, 2).reshape(1, LANES)
    span = (BAND_CHUNKS + 1) * CHUNK
    rel_p = jnp.arange(span)[None, :] - BAND_CHUNKS * CHUNK - jnp.arange(CHUNK)[:, None]
    bias_p = _pair_bias(rel_p, rel_bias)
    key_off = jnp.tile(jnp.arange(span), 2)[None, None, None, :]
    first_valid = ((BAND_CHUNKS - jnp.arange(BAND_CHUNKS + 1)) * CHUNK)[:, None, None, None]
    bias_p = jnp.where(key_off >= first_valid, bias_p[None], -jnp.inf)
    sink_p = _pair_sinks(attn_sinks[l], CHUNK)
    rows = cache_k.shape[2]
    rel_s = jnp.arange(rows + dseq)[None, :] - rows - jnp.arange(dseq)[:, None]
    bias_s = _pair_bias(rel_s, rel_bias)[None]
    sink_s = _pair_sinks(attn_sinks[l], dseq)

    wglu = w_glu[l].astype(BF16)
    bglu = b_glu[l].astype(F32).reshape(1, sw)
    woa = w_out_a[l].astype(BF16)
    wob = w_out_b[l].astype(BF16)
    wo = w_o[l].astype(BF16)
    merge_cols = (c_za, c_zb, c_ga, c_gb)

    xp2 = x_prompt.reshape(nb * seq, d)
    xs2 = x_sample.reshape(db * dseq, d)
    hp, hs = _inproj(xp2, xs2, gain, w_in[l], tm=min(256, nb * seq), cast_rows=64)
    hp3 = hp.reshape(nb, seq, -1)
    yp, hfin_p = _s5(hp3, jnp.zeros((nb, 2 * n_state), F32), lam_b, bmat, cmat, d_slab, tl=min(64, seq), n_sub=2)
    op, pk, pv = _band_attn(hp3, c_q, c_k, c_v, qgain2, kgain2, bias_p, sink_p, _pair_ones(span), qb=min(2048, seq))
    y_p = _merge(xp2, yp.reshape(nb * seq, sw), op.reshape(nb * seq, aw), hp, merge_cols,
                 wglu, bglu, woa, wob, wo, tm=512, n_part=2).reshape(nb, seq, d)

    hs3 = hs.reshape(db, dseq, -1)
    h0 = jnp.concatenate([state_ssm_re[l].reshape(db, n_state), state_ssm_im[l].reshape(db, n_state)],
                         axis=1).astype(F32)
    ys, hfin_s = _s5(hs3, h0, lam_b, bmat, cmat, d_slab, tl=dseq, n_sub=1)
    ck = cache_k[l].reshape(db, rows, kvw).astype(F32)
    cv = cache_v[l].reshape(db, rows, kvw).astype(F32)
    os_, sk, sv = _cached_attn(hs3, c_q, c_k, c_v, ck, cv, qgain2, kgain2, bias_s, sink_s, _pair_ones(rows + dseq))
    y_s = _merge(xs2, ys.reshape(db * dseq, sw), os_.reshape(db * dseq, aw), hs, merge_cols,
                 wglu, bglu, woa, wob, wo, tm=db * dseq, n_part=1).reshape(db, dseq, d)

    st = lambda h, nbb: (h[:, :n_state].reshape(1, nbb, n_groups, n_st), h[:, n_state:].reshape(1, nbb, n_groups, n_st))
    p_re, p_im = st(hfin_p, nb)
    s_re, s_im = st(hfin_s, db)
    kvshape = lambda a: a.reshape(1, a.shape[0], a.shape[1], 2, HEAD_DIM)
    return (y_p, y_s, p_re, p_im, kvshape(pk), kvshape(pv), s_re, s_im, kvshape(sk), kvshape(sv))
```

```python
import functools
import math

import jax
import jax.numpy as jnp
from jax import lax
from jax.experimental import pallas as pl
from jax.experimental.pallas import tpu as pltpu

F32 = jnp.float32
BF16 = jnp.bfloat16

LANES = 128
SUBLANES = 8
VMEM_LIMIT = 56 * 1024 * 1024

CHUNK = 64
HEAD_DIM = 64
GQA_GROUP = 8
SSM_GROUP = 16
SSM_STATE = 64
WINDOW = 128
BAND_CHUNKS = 2
N_BUCKETS = 32
MAX_DISTANCE = 128
EPS = 1e-6
SCALE = HEAD_DIM ** -0.5
LOG2E = math.log2(math.e)
SLAB_GROUPS = LANES // SSM_GROUP
SLAB_STATES = SLAB_GROUPS * SSM_STATE
SCAN_LANES = 1024


def _inproj_body(xp_ref, xs_ref, g_ref, w_hbm, hp_ref, hs_ref, w_ref, stage_ref, sem, *, n_tile, cast_rows):
    i = pl.program_id(0)
    d = w_ref.shape[0]
    n_chunk = d // cast_rows

    def chunk_copy(c, slot):
        return pltpu.make_async_copy(w_hbm.at[pl.ds(c * cast_rows, cast_rows), :], stage_ref.at[slot], sem.at[slot])

    @pl.when(i == 0)
    def _():
        chunk_copy(0, 0).start()

        def cast(c, carry):
            slot = c % 2

            @pl.when(c + 1 < n_chunk)
            def _():
                chunk_copy(c + 1, 1 - slot).start()

            chunk_copy(c, slot).wait()
            w_ref[pl.ds(pl.multiple_of(c * cast_rows, cast_rows), cast_rows), :] = stage_ref[slot].astype(BF16)
            return carry

        lax.fori_loop(0, n_chunk, cast, 0)

    def project(x_ref, o_ref):
        x = x_ref[...]
        ms = jnp.mean(x * x, axis=-1, keepdims=True)
        xn = (x * lax.rsqrt(ms + EPS) * g_ref[...]).astype(BF16)
        o_ref[...] = jnp.dot(xn, w_ref[...], preferred_element_type=F32).astype(BF16)

    @pl.when(i < n_tile)
    def _():
        project(xp_ref, hp_ref)

    @pl.when(i == n_tile)
    def _():
        project(xs_ref, hs_ref)


def _inproj(xp2, xs2, gain, w_f32, tm, cast_rows):
    n, d = xp2.shape
    ns = xs2.shape[0]
    width = w_f32.shape[1]
    n_tile = n // tm
    last = lambda i: (jnp.minimum(i, n_tile - 1), 0)
    return pl.pallas_call(
        functools.partial(_inproj_body, n_tile=n_tile, cast_rows=cast_rows),
        grid=(n_tile + 1,),
        in_specs=[pl.BlockSpec((tm, d), last),
                  pl.BlockSpec((ns, d), lambda i: (0, 0)),
                  pl.BlockSpec((1, d), lambda i: (0, 0)),
                  pl.BlockSpec(memory_space=pl.ANY)],
        out_specs=[pl.BlockSpec((tm, width), last),
                   pl.BlockSpec((ns, width), lambda i: (0, 0))],
        out_shape=[jax.ShapeDtypeStruct((n, width), BF16),
                   jax.ShapeDtypeStruct((ns, width), BF16)],
        scratch_shapes=[pltpu.VMEM((d, width), BF16),
                        pltpu.VMEM((2, cast_rows, width), F32),
                        pltpu.SemaphoreType.DMA((2,))],
        compiler_params=pltpu.CompilerParams(
            dimension_semantics=("arbitrary",), vmem_limit_bytes=VMEM_LIMIT),
        name="inproj",
    )(xp2, xs2, gain, w_f32)


def _s5_body(u_ref, h0_ref, lam_ref, bmat_ref, cmat_ref, d_ref, y_ref, hout_ref,
             utb_ref, ytb_ref, bu_ref, hst_ref, *, nb, tl, n_slab, n_state, n_sub):
    sub_t = tl // n_sub
    sub_m = sub_t * nb
    n_pass = n_state // SCAN_LANES

    @pl.when(pl.program_id(0) == 0)
    def _():
        hst_ref[...] = h0_ref[...]

    for s in range(n_slab):
        for b in range(nb):
            utb_ref[s, pl.ds(b, tl, stride=nb), :] = u_ref[b, :, s * LANES:(s + 1) * LANES].astype(F32)

    def re_cols(s, width=SLAB_STATES):
        return slice(s * width, (s + 1) * width)

    def im_cols(s, width=SLAB_STATES):
        return slice(n_state + s * width, n_state + (s + 1) * width)

    def project_in(k):
        rows = slice(k * sub_m, (k + 1) * sub_m)
        for s in range(n_slab):
            bu = jnp.dot(utb_ref[s, rows, :].astype(BF16), bmat_ref[s], preferred_element_type=F32)
            bu_ref[rows, re_cols(s)] = bu[:, :SLAB_STATES]
            bu_ref[rows, im_cols(s)] = bu[:, SLAB_STATES:]

    def scan(k, state):
        out = []
        for c, (hr, hi) in enumerate(state):
            re, im = re_cols(c, SCAN_LANES), im_cols(c, SCAN_LANES)
            lr = lam_ref[0, :, re]
            li = lam_ref[1, :, re]
            for t in range(k * sub_t, (k + 1) * sub_t):
                r = slice(t * nb, (t + 1) * nb)
                hr, hi = lr * hr - li * hi + bu_ref[r, re], lr * hi + li * hr + bu_ref[r, im]
                bu_ref[r, re] = hr
                bu_ref[r, im] = hi
            out.append((hr, hi))
        return out

    def project_out(k):
        rows = slice(k * sub_m, (k + 1) * sub_m)
        for s in range(n_slab):
            y = (jnp.dot(bu_ref[rows, re_cols(s)].astype(BF16), cmat_ref[s, :SLAB_STATES, :], preferred_element_type=F32)
                 + jnp.dot(bu_ref[rows, im_cols(s)].astype(BF16), cmat_ref[s, SLAB_STATES:, :], preferred_element_type=F32))
            ytb_ref[s, rows, :] = y + d_ref[s] * utb_ref[s, rows, :]

    state = [(hst_ref[:, re_cols(c, SCAN_LANES)], hst_ref[:, im_cols(c, SCAN_LANES)]) for c in range(n_pass)]
    project_in(0)
    for k in range(n_sub):
        if k + 1 < n_sub:
            project_in(k + 1)
        state = scan(k, state)
        if k >= 1:
            project_out(k - 1)
    project_out(n_sub - 1)
    for c, (hr, hi) in enumerate(state):
        hst_ref[:, re_cols(c, SCAN_LANES)] = hr
        hst_ref[:, im_cols(c, SCAN_LANES)] = hi
    hout_ref[...] = hst_ref[...]

    for s in range(n_slab):
        for b in range(nb):
            y_ref[b, :, s * LANES:(s + 1) * LANES] = ytb_ref[s, pl.ds(b, tl, stride=nb), :].astype(BF16)


def _s5(h3, h0, lam_b, bmat, cmat, d_slab, tl, n_sub):
    nb, seq, _ = h3.shape
    n_slab = bmat.shape[0]
    width = n_slab * LANES
    n_state = n_slab * SLAB_STATES
    m = tl * nb
    body = functools.partial(_s5_body, nb=nb, tl=tl, n_slab=n_slab, n_state=n_state, n_sub=n_sub)
    const = lambda *shape: pl.BlockSpec(shape, lambda i: (0,) * len(shape))
    return pl.pallas_call(
        body,
        grid=(seq // tl,),
        in_specs=[pl.BlockSpec((nb, tl, width), lambda i: (0, i, 0)),
                  const(nb, 2 * n_state),
                  const(2, nb, n_state),
                  const(n_slab, LANES, 2 * SLAB_STATES),
                  const(n_slab, 2 * SLAB_STATES, LANES),
                  const(n_slab, 1, LANES)],
        out_specs=[pl.BlockSpec((nb, tl, width), lambda i: (0, i, 0)),
                   const(nb, 2 * n_state)],
        out_shape=[jax.ShapeDtypeStruct((nb, seq, width), BF16),
                   jax.ShapeDtypeStruct((nb, 2 * n_state), F32)],
        scratch_shapes=[pltpu.VMEM((n_slab, m, LANES), F32),
                        pltpu.VMEM((n_slab, m, LANES), F32),
                        pltpu.VMEM((m, 2 * n_state), F32),
                        pltpu.VMEM((nb, 2 * n_state), F32)],
        compiler_params=pltpu.CompilerParams(
            dimension_semantics=("arbitrary",), vmem_limit_bytes=VMEM_LIMIT),
        name="s5",
    )(h3, h0, lam_b, bmat, cmat, d_slab)


def _lo_lanes():
    return lax.broadcasted_iota(jnp.int32, (1, LANES), 1) < HEAD_DIM


def _head_norm(t, gain2_scaled):
    lo = _lo_lanes()
    sq = t * t
    ss_lo = jnp.sum(jnp.where(lo, sq, 0.0), axis=-1, keepdims=True)
    ss_hi = jnp.sum(jnp.where(lo, 0.0, sq), axis=-1, keepdims=True)
    return t * lax.rsqrt(jnp.where(lo, ss_lo, ss_hi) + HEAD_DIM * EPS) * gain2_scaled


def _pair_blockdiag(t):
    lo = _lo_lanes()
    r = pltpu.roll(t, HEAD_DIM, 1)
    return ((jnp.where(lo, t, 0.0), jnp.where(lo, 0.0, r)),
            (jnp.where(lo, r, 0.0), jnp.where(lo, 0.0, t)))


def _rowmax(parts):
    full, out = None, None
    for p in parts:
        w = p.shape[1]
        for c0 in range(0, w - w % LANES, LANES):
            blk = p[:, c0:c0 + LANES]
            full = blk if full is None else jnp.maximum(full, blk)
        if w % LANES:
            r = jnp.max(p[:, w - w % LANES:], axis=-1, keepdims=True)
            out = r if out is None else jnp.maximum(out, r)
    if full is not None:
        r = jnp.max(full, axis=-1, keepdims=True)
        out = r if out is None else jnp.maximum(out, r)
    return out


def _softmax_numerators(sc, sinks):
    s = sc.shape[1] // 2
    b = (s // LANES) * LANES
    left, mid, right = sc[:, :b], sc[:, b:b + LANES], sc[:, b + LANES:]
    in0 = lax.broadcasted_iota(jnp.int32, (1, LANES), 1) < (s - b)
    m0 = _rowmax([left, jnp.where(in0, mid, sinks[0])])
    m1 = _rowmax([jnp.where(in0, sinks[1], mid), right])
    e = [jnp.exp2(left[:, c:c + LANES] - m0) for c in range(0, b, LANES)]
    e.append(jnp.exp2(mid - jnp.where(in0, m0, m1)))
    e += [jnp.exp2(right[:, c:c + LANES] - m1) for c in range(0, right.shape[1], LANES)]
    sink_share = jnp.exp2(sinks[2] - jnp.where(_lo_lanes(), m0, m1))
    return jnp.concatenate(e, axis=1).astype(BF16), sink_share


def _scores(qn, kbd, bias):
    return lax.dot_general(qn, kbd, (((1,), (1,)), ((), ())), preferred_element_type=F32) + bias


def _weighted_values(e, vbd1, sink_share):
    od = jnp.dot(e, vbd1, preferred_element_type=F32)
    return od[:, :LANES] / (od[:, LANES:] + sink_share)


def _band_attn_body(q_ref, k_ref, v_ref, qg_ref, kg_ref, bias_ref, sink_ref, ones_ref, o_ref, pk_ref, pv_ref,
                    kvbd_ref, qn_ref, sc0_ref, sc1_ref, e0_ref, e1_ref, share0_ref, share1_ref, *, seq, qb):
    j = pl.program_id(1)
    pad = BAND_CHUNKS * CHUNK
    span = pad + CHUNK
    rows = min(512, seq)
    n_pair = GQA_GROUP // 2
    n_chunk = qb // CHUNK
    sc_refs, e_refs, share_refs = (sc0_ref, sc1_ref), (e0_ref, e1_ref), (share0_ref, share1_ref)
    assert n_chunk >= 4 and n_chunk % 2 == 0

    @pl.when(j == 0)
    def _():
        zeros = jnp.zeros((pad, LANES), BF16)
        for a in range(8):
            kvbd_ref[a, 0:pad, :] = zeros

        def fill(i, carry):
            r0 = pl.multiple_of(i * rows, rows)
            kn = _head_norm(k_ref[0, pl.ds(r0, rows), :].astype(F32), kg_ref[...])
            for is_v, t in enumerate((kn, v_ref[0, pl.ds(r0, rows), :].astype(F32))):
                for kv, halves in enumerate(_pair_blockdiag(t)):
                    for tb, half in enumerate(halves):
                        kvbd_ref[4 * is_v + 2 * kv + tb, pl.ds(pad + r0, rows), :] = half.astype(BF16)
            return carry

        lax.fori_loop(0, seq // rows, fill, 0)
        pk_ref[0] = _head_norm(k_ref[0, seq - WINDOW:seq, :].astype(F32), kg_ref[...])
        pv_ref[0] = v_ref[0, seq - WINDOW:seq, :].astype(F32)

    def lanes_of(kv, i):
        return slice((kv * n_pair + i) * LANES, (kv * n_pair + i + 1) * LANES)

    def stage_norm(c):
        q0 = pl.multiple_of(c * CHUNK, CHUNK)
        for kv in range(2):
            for i in range(n_pair):
                qn_ref[c, kv, i * CHUNK:(i + 1) * CHUNK, :] = _head_norm(
                    q_ref[0, pl.ds(q0, CHUNK), lanes_of(kv, i)].astype(F32), qg_ref[...]).astype(BF16)

    def key_rows(c):
        return pl.ds(pl.multiple_of((j * n_chunk + c) * CHUNK, CHUNK), span)

    def stage_scores(c, slot):
        variant = jnp.minimum(j * n_chunk + c, BAND_CHUNKS)
        for kv in range(2):
            kbd = jnp.concatenate([kvbd_ref[2 * kv, key_rows(c), :], kvbd_ref[2 * kv + 1, key_rows(c), :]], axis=0)
            sc_refs[slot][kv] = _scores(qn_ref[c, kv], kbd, bias_ref[variant, kv])

    def stage_exp(slot):
        for kv in range(2):
            e_refs[slot][kv], share_refs[slot][kv] = _softmax_numerators(sc_refs[slot][kv], sink_ref[kv])

    def stage_values(c, slot):
        q0 = pl.multiple_of(c * CHUNK, CHUNK)
        for kv in range(2):
            vbd = jnp.concatenate([kvbd_ref[4 + 2 * kv, key_rows(c), :], kvbd_ref[4 + 2 * kv + 1, key_rows(c), :]], axis=0)
            o = _weighted_values(e_refs[slot][kv], jnp.concatenate([vbd, ones_ref[...]], axis=1), share_refs[slot][kv])
            for i in range(n_pair):
                o_ref[0, pl.ds(q0, CHUNK), lanes_of(kv, i)] = o[i * CHUNK:(i + 1) * CHUNK].astype(BF16)

    def iteration(i, parity):
        live = (lambda c: 0 <= c < n_chunk) if isinstance(i, int) else (lambda c: True)
        if live(i - 3):
            stage_values(i - 3, 1 - parity)
        if live(i - 2):
            stage_exp(parity)
        if live(i - 1):
            stage_scores(i - 1, 1 - parity)
        if live(i):
            stage_norm(i)

    for i in range(4):
        iteration(i, i % 2)

    def steady(p, carry):
        iteration(4 + 2 * p, 0)
        iteration(5 + 2 * p, 1)
        return carry

    lax.fori_loop(0, (n_chunk - 4) // 2, steady, 0)
    for i in range(n_chunk, n_chunk + 3):
        iteration(i, i % 2)


def _band_attn(h3, qcol, kcol, vcol, qgain2, kgain2, bias, sinkcol, ones_bd, qb):
    nb, seq, _ = h3.shape
    aw = 2 * GQA_GROUP * HEAD_DIM
    pad = BAND_CHUNKS * CHUNK
    body = functools.partial(_band_attn_body, seq=seq, qb=qb)
    const = lambda *shape: pl.BlockSpec(shape, lambda b, j: (0,) * len(shape))
    return pl.pallas_call(
        body,
        grid=(nb, seq // qb),
        in_specs=[pl.BlockSpec((1, qb, aw), lambda b, j: (b, j, qcol // aw)),
                  pl.BlockSpec((1, seq, LANES), lambda b, j: (b, 0, kcol // LANES)),
                  pl.BlockSpec((1, seq, LANES), lambda b, j: (b, 0, vcol // LANES)),
                  const(1, LANES), const(1, LANES),
                  const(*bias.shape), const(*sinkcol.shape), const(*ones_bd.shape)],
        out_specs=[pl.BlockSpec((1, qb, aw), lambda b, j: (b, j, 0)),
                   pl.BlockSpec((1, WINDOW, LANES), lambda b, j: (b, 0, 0)),
                   pl.BlockSpec((1, WINDOW, LANES), lambda b, j: (b, 0, 0))],
        out_shape=[jax.ShapeDtypeStruct((nb, seq, aw), BF16),
                   jax.ShapeDtypeStruct((nb, WINDOW, LANES), F32),
                   jax.ShapeDtypeStruct((nb, WINDOW, LANES), F32)],
        scratch_shapes=[pltpu.VMEM((8, seq + pad, LANES), BF16),
                        pltpu.VMEM((qb // CHUNK, 2, (GQA_GROUP // 2) * CHUNK, LANES), BF16),
                        *[pltpu.VMEM((2, (GQA_GROUP // 2) * CHUNK, 2 * (pad + CHUNK)), F32)] * 2,
                        *[pltpu.VMEM((2, (GQA_GROUP // 2) * CHUNK, 2 * (pad + CHUNK)), BF16)] * 2,
                        *[pltpu.VMEM((2, (GQA_GROUP // 2) * CHUNK, LANES), F32)] * 2],
        compiler_params=pltpu.CompilerParams(
            dimension_semantics=("parallel", "arbitrary"), vmem_limit_bytes=VMEM_LIMIT),
        name="band_attn",
    )(h3, h3, h3, qgain2, kgain2, bias, sinkcol, ones_bd)


def _cached_attn_body(q_ref, k_ref, v_ref, ck_ref, cv_ref, qg_ref, kg_ref, bias_ref, sink_ref, ones_ref,
                      o_ref, sk_ref, sv_ref):
    t = q_ref.shape[1]
    n_pair = GQA_GROUP // 2
    kn = _head_norm(k_ref[0].astype(F32), kg_ref[...])
    vn = v_ref[0].astype(F32)
    sk_ref[0] = kn
    sv_ref[0] = vn
    kbds = _pair_blockdiag(jnp.concatenate([ck_ref[0], kn], axis=0))
    vbds = _pair_blockdiag(jnp.concatenate([cv_ref[0], vn], axis=0))
    for kv in range(2):
        qn = jnp.concatenate(
            [_head_norm(q_ref[0, :, (kv * n_pair + i) * LANES:(kv * n_pair + i + 1) * LANES].astype(F32), qg_ref[...])
             for i in range(n_pair)], axis=0).astype(BF16)
        kbd = jnp.concatenate(kbds[kv], axis=0).astype(BF16)
        vbd1 = jnp.concatenate([jnp.concatenate(vbds[kv], axis=0).astype(BF16), ones_ref[...]], axis=1)
        e, sink_share = _softmax_numerators(_scores(qn, kbd, bias_ref[0, kv]), sink_ref[kv])
        o = _weighted_values(e, vbd1, sink_share)
        for i in range(n_pair):
            o_ref[0, :, (kv * n_pair + i) * LANES:(kv * n_pair + i + 1) * LANES] = o[i * t:(i + 1) * t].astype(BF16)


def _cached_attn(h3, qcol, kcol, vcol, cache_k, cache_v, qgain2, kgain2, bias, sinkcol, ones_bd):
    nb, t, _ = h3.shape
    aw = 2 * GQA_GROUP * HEAD_DIM
    r = cache_k.shape[1]
    const = lambda *shape: pl.BlockSpec(shape, lambda b: (0,) * len(shape))
    return pl.pallas_call(
        _cached_attn_body,
        grid=(nb,),
        in_specs=[pl.BlockSpec((1, t, aw), lambda b: (b, 0, qcol // aw)),
                  pl.BlockSpec((1, t, LANES), lambda b: (b, 0, kcol // LANES)),
                  pl.BlockSpec((1, t, LANES), lambda b: (b, 0, vcol // LANES)),
                  pl.BlockSpec((1, r, LANES), lambda b: (b, 0, 0)),
                  pl.BlockSpec((1, r, LANES), lambda b: (b, 0, 0)),
                  const(1, LANES), const(1, LANES),
                  const(*bias.shape), const(*sinkcol.shape), const(*ones_bd.shape)],
        out_specs=[pl.BlockSpec((1, t, aw), lambda b: (b, 0, 0)),
                   pl.BlockSpec((1, t, LANES), lambda b: (b, 0, 0)),
                   pl.BlockSpec((1, t, LANES), lambda b: (b, 0, 0))],
        out_shape=[jax.ShapeDtypeStruct((nb, t, aw), BF16),
                   jax.ShapeDtypeStruct((nb, t, LANES), F32),
                   jax.ShapeDtypeStruct((nb, t, LANES), F32)],
        compiler_params=pltpu.CompilerParams(
            dimension_semantics=("parallel",), vmem_limit_bytes=VMEM_LIMIT),
        name="cached_attn",
    )(h3, h3, h3, cache_k, cache_v, qgain2, kgain2, bias, sinkcol, ones_bd)


def _sigmoid(x):
    return 1.0 / (1.0 + jnp.exp2(x * -LOG2E))


def _merge_body(x_ref, y_ref, za_ref, o_ref, zb_ref, ga_ref, gb_ref,
                wglu_ref, bglu_ref, woa_ref, wob_ref, wo_ref, out_ref, *, n_part):
    part = x_ref.shape[0] // n_part
    for p in range(n_part):
        r = slice(p * part, (p + 1) * part)
        y = y_ref[r, :].astype(F32)
        g = 0.5 * y * (1.0 + lax.erf(y * (2.0 ** -0.5)))
        gl = jnp.dot(g.astype(BF16), wglu_ref[...], preferred_element_type=F32) + bglu_ref[...]
        za = za_ref[r, :].astype(F32)
        br_a = g * _sigmoid(gl) * (za * _sigmoid(za))
        zb = zb_ref[r, :].astype(F32)
        br_b = o_ref[r, :].astype(F32) * (zb * _sigmoid(zb))
        mixed = (_sigmoid(ga_ref[r, :].astype(F32)) * jnp.dot(br_a.astype(BF16), woa_ref[...], preferred_element_type=F32)
                 + _sigmoid(gb_ref[r, :].astype(F32)) * jnp.dot(br_b.astype(BF16), wob_ref[...], preferred_element_type=F32))
        out_ref[r, :] = x_ref[r, :] + jnp.dot(mixed.astype(BF16), wo_ref[...], preferred_element_type=F32)


def _merge(x2, y2, o2, h2, cols, wglu, bglu, woa, wob, wo, tm, n_part):
    n, d = x2.shape
    sw = y2.shape[1]
    aw = o2.shape[1]
    za_col, zb_col, ga_col, gb_col = cols
    row = lambda width, col: pl.BlockSpec((pl.Element(tm), pl.Element(width)), lambda i: (i * tm, col))
    weight = lambda a: pl.BlockSpec(a.shape, lambda i: (0, 0), pipeline_mode=pl.Buffered(1))
    return pl.pallas_call(
        functools.partial(_merge_body, n_part=n_part),
        grid=(n // tm,),
        in_specs=[row(d, 0), row(sw, 0), row(sw, za_col), row(aw, 0), row(aw, zb_col),
                  row(d, ga_col), row(d, gb_col),
                  weight(wglu), weight(bglu), weight(woa), weight(wob), weight(wo)],
        out_specs=row(d, 0),
        out_shape=jax.ShapeDtypeStruct((n, d), F32),
        compiler_params=pltpu.CompilerParams(
            dimension_semantics=("parallel",), vmem_limit_bytes=VMEM_LIMIT),
        name="merge",
    )(x2, y2, h2, o2, h2, h2, h2, wglu, bglu, woa, wob, wo)


def _discretize(a_re, a_im, log_dt, b_re, b_im):
    dt = jnp.exp(log_dt)[:, None]
    mag = jnp.exp(a_re * dt)
    ang = a_im * dt
    lam_re = mag * jnp.cos(ang)
    lam_im = mag * jnp.sin(ang)
    den = a_re * a_re + a_im * a_im
    cr = ((lam_re - 1.0) * a_re + lam_im * a_im) / den
    ci = (lam_im * a_re - (lam_re - 1.0) * a_im) / den
    bb_re = cr[..., None] * b_re - ci[..., None] * b_im
    bb_im = cr[..., None] * b_im + ci[..., None] * b_re
    return lam_re, lam_im, bb_re, bb_im


def _slab_blockdiag(t):
    n_slab, g, a, b = t.shape
    same_group = jnp.eye(g, dtype=bool)[None, :, None, :, None]
    return jnp.where(same_group, t[:, :, :, None, :], 0.0).reshape(n_slab, g * a, g * b)


def _t5_bucket(rel):
    half = N_BUCKETS // 2
    n = -rel
    ret = jnp.where(n < 0, half, 0)
    n = jnp.abs(n)
    max_exact = half // 2
    nf = jnp.maximum(n, 1).astype(F32)
    large = max_exact + (jnp.log(nf / max_exact) / math.log(MAX_DISTANCE / max_exact)
                         * (half - max_exact)).astype(jnp.int32)
    large = jnp.minimum(large, half - 1)
    return ret + jnp.where(n < max_exact, n, large)


def _pair_bias(rel, table):
    t, s = rel.shape
    onehot = (_t5_bucket(rel)[..., None] == jnp.arange(N_BUCKETS)).astype(F32)
    b = jnp.einsum("tsn,nh->tsh", onehot, table.astype(F32) * LOG2E, precision=lax.Precision.HIGHEST)
    b = b.reshape(t, s, 2, GQA_GROUP // 2, 2)
    return jnp.transpose(b, (2, 3, 0, 4, 1)).reshape(2, (GQA_GROUP // 2) * t, 2 * s)


def _pair_sinks(sinks, t):
    sk = jnp.transpose(sinks.astype(F32).reshape(2, GQA_GROUP // 2, 2), (0, 2, 1))[:, :, :, None, None]
    sk = jnp.broadcast_to(sk, (2, 2, GQA_GROUP // 2, t, LANES)).reshape(2, 2, (GQA_GROUP // 2) * t, LANES)
    lo = jnp.arange(LANES) < HEAD_DIM
    return jnp.concatenate([sk, jnp.where(lo, sk[:, 0], sk[:, 1])[:, None]], axis=1) * LOG2E


def _pair_ones(s):
    first = (jnp.arange(2 * s) < s)[:, None]
    lo = (jnp.arange(LANES) < HEAD_DIM)[None, :]
    return (first == lo).astype(BF16)


def kernel(x_prompt, x_sample, cache_k, cache_v, state_ssm_re, state_ssm_im, norm_gain, w_in, ssm_a_re, ssm_a_im, ssm_log_dt, ssm_b_re, ssm_b_im, ssm_c_re, ssm_c_im, ssm_d, w_glu, b_glu, q_gain, k_gain, attn_sinks, rel_bias, w_out_a, w_out_b, w_o):
    nb, seq, d = x_prompt.shape
    db, dseq, _ = x_sample.shape
    n_groups, n_st = ssm_a_re.shape[1:]
    sw = n_groups * SSM_GROUP
    aw = 2 * GQA_GROUP * HEAD_DIM
    kvw = 2 * HEAD_DIM
    n_slab = sw // LANES
    n_state = n_groups * n_st
    l = 0

    c_u, c_za, c_q = 0, sw, 2 * sw
    c_k = c_q + aw
    c_v = c_k + kvw
    c_zb = c_v + kvw
    c_ga = c_zb + aw
    c_gb = c_ga + d
    gain = norm_gain[l].astype(F32).reshape(1, d)

    lam_re, lam_im, bb_re, bb_im = _discretize(
        ssm_a_re[l].astype(F32), ssm_a_im[l].astype(F32), ssm_log_dt[l].astype(F32),
        ssm_b_re[l].astype(F32), ssm_b_im[l].astype(F32))
    lam_b = jnp.broadcast_to(jnp.stack([lam_re.reshape(-1), lam_im.reshape(-1)])[:, None, :], (2, nb, n_state))
    slab = lambda t: t.reshape(n_slab, SLAB_GROUPS, *t.shape[1:])
    bmat = jnp.concatenate([_slab_blockdiag(jnp.swapaxes(slab(bb_re), 2, 3)),
                            _slab_blockdiag(jnp.swapaxes(slab(bb_im), 2, 3))], axis=2).astype(BF16)
    cmat = jnp.concatenate([_slab_blockdiag(jnp.swapaxes(slab(ssm_c_re[l].astype(F32)), 2, 3)),
                            _slab_blockdiag(jnp.swapaxes(slab(-ssm_c_im[l].astype(F32)), 2, 3))], axis=1).astype(BF16)
    d_slab = ssm_d[l].astype(F32).reshape(n_slab, 1, LANES)

    qgain2 = jnp.tile(q_gain[l].astype(F32) * (HEAD_DIM ** 0.5 * SCALE * LOG2E), 2).reshape(1, LANES)
    kgain2 = jnp.tile(k_gain[l].astype(F32) * HEAD_DIM ** 0.5, 2).reshape(1, LANES)
    span = (BAND_CHUNKS + 1) * CHUNK
    rel_p = jnp.arange(span)[None, :] - BAND_CHUNKS * CHUNK - jnp.arange(CHUNK)[:, None]
    bias_p = _pair_bias(rel_p, rel_bias)
    key_off = jnp.tile(jnp.arange(span), 2)[None, None, None, :]
    first_valid = ((BAND_CHUNKS - jnp.arange(BAND_CHUNKS + 1)) * CHUNK)[:, None, None, None]
    bias_p = jnp.where(key_off >= first_valid, bias_p[None], -jnp.inf)
    sink_p = _pair_sinks(attn_sinks[l], CHUNK)
    rows = cache_k.shape[2]
    rel_s = jnp.arange(rows + dseq)[None, :] - rows - jnp.arange(dseq)[:, None]
    bias_s = _pair_bias(rel_s, rel_bias)[None]
    sink_s = _pair_sinks(attn_sinks[l], dseq)

    wglu = w_glu[l].astype(BF16)
    bglu = b_glu[l].astype(F32).reshape(1, sw)
    woa = w_out_a[l].astype(BF16)
    wob = w_out_b[l].astype(BF16)
    wo = w_o[l].astype(BF16)
    merge_cols = (c_za, c_zb, c_ga, c_gb)

    xp2 = x_prompt.reshape(nb * seq, d)
    xs2 = x_sample.reshape(db * dseq, d)
    hp, hs = _inproj(xp2, xs2, gain, w_in[l], tm=min(256, nb * seq), cast_rows=64)
    hp3 = hp.reshape(nb, seq, -1)
    yp, hfin_p = _s5(hp3, jnp.zeros((nb, 2 * n_state), F32), lam_b, bmat, cmat, d_slab, tl=min(64, seq), n_sub=2)
    op, pk, pv = _band_attn(hp3, c_q, c_k, c_v, qgain2, kgain2, bias_p, sink_p, _pair_ones(span), qb=min(2048, seq))
    y_p = _merge(xp2, yp.reshape(nb * seq, sw), op.reshape(nb * seq, aw), hp, merge_cols,
                 wglu, bglu, woa, wob, wo, tm=512, n_part=2).reshape(nb, seq, d)

    hs3 = hs.reshape(db, dseq, -1)
    h0 = jnp.concatenate([state_ssm_re[l].reshape(db, n_state), state_ssm_im[l].reshape(db, n_state)],
                         axis=1).astype(F32)
    ys, hfin_s = _s5(hs3, h0, lam_b, bmat, cmat, d_slab, tl=dseq, n_sub=1)
    ck = cache_k[l].reshape(db, rows, kvw).astype(F32)
    cv = cache_v[l].reshape(db, rows, kvw).astype(F32)
    os_, sk, sv = _cached_attn(hs3, c_q, c_k, c_v, ck, cv, qgain2, kgain2, bias_s, sink_s, _pair_ones(rows + dseq))
    y_s = _merge(xs2, ys.reshape(db * dseq, sw), os_.reshape(db * dseq, aw), hs, merge_cols,
                 wglu, bglu, woa, wob, wo, tm=db * dseq, n_part=1).reshape(db, dseq, d)

    st = lambda h, nbb: (h[:, :n_state].reshape(1, nbb, n_groups, n_st), h[:, n_state:].reshape(1, nbb, n_groups, n_st))
    p_re, p_im = st(hfin_p, nb)
    s_re, s_im = st(hfin_s, db)
    kvshape = lambda a: a.reshape(1, a.shape[0], a.shape[1], 2, HEAD_DIM)
    return (y_p, y_s, p_re, p_im, kvshape(pk), kvshape(pv), s_re, s_im, kvshape(sk), kvshape(sv))
```

```python
import functools
import math

import jax
import jax.numpy as jnp
from jax import lax
from jax.experimental import pallas as pl
from jax.experimental.pallas import tpu as pltpu

F32 = jnp.float32
BF16 = jnp.bfloat16

LANES = 128
SUBLANES = 8
VMEM_LIMIT = 56 * 1024 * 1024

CHUNK = 64
HEAD_DIM = 64
GQA_GROUP = 8
SSM_GROUP = 16
SSM_STATE = 64
WINDOW = 128
BAND_CHUNKS = 2
N_BUCKETS = 32
MAX_DISTANCE = 128
EPS = 1e-6
SCALE = HEAD_DIM ** -0.5
LOG2E = math.log2(math.e)
SLAB_GROUPS = LANES // SSM_GROUP
SLAB_STATES = SLAB_GROUPS * SSM_STATE
SCAN_LANES = 1024


def _inproj_body(xp_ref, xs_ref, g_ref, w_hbm, hp_ref, hs_ref, w_ref, stage_ref, sem, *, n_tile, cast_rows):
    i = pl.program_id(0)
    d = w_ref.shape[0]
    n_chunk = d // cast_rows

    def chunk_copy(c, slot):
        return pltpu.make_async_copy(w_hbm.at[pl.ds(c * cast_rows, cast_rows), :], stage_ref.at[slot], sem.at[slot])

    @pl.when(i == 0)
    def _():
        chunk_copy(0, 0).start()

        def cast(c, carry):
            slot = c % 2

            @pl.when(c + 1 < n_chunk)
            def _():
                chunk_copy(c + 1, 1 - slot).start()

            chunk_copy(c, slot).wait()
            w_ref[pl.ds(pl.multiple_of(c * cast_rows, cast_rows), cast_rows), :] = stage_ref[slot].astype(BF16)
            return carry

        lax.fori_loop(0, n_chunk, cast, 0)

    def project(x_ref, o_ref):
        x = x_ref[...]
        r = lax.rsqrt(jnp.mean(x * x, axis=-1, keepdims=True) + EPS)
        xg = (x * g_ref[...]).astype(BF16)
        o_ref[...] = (jnp.dot(xg, w_ref[...], preferred_element_type=F32) * r).astype(BF16)

    @pl.when(i < n_tile)
    def _():
        project(xp_ref, hp_ref)

    @pl.when(i == n_tile)
    def _():
        project(xs_ref, hs_ref)


def _inproj(xp2, xs2, gain, w_f32, tm, cast_rows):
    n, d = xp2.shape
    ns = xs2.shape[0]
    width = w_f32.shape[1]
    n_tile = n // tm
    last = lambda i: (jnp.minimum(i, n_tile - 1), 0)
    return pl.pallas_call(
        functools.partial(_inproj_body, n_tile=n_tile, cast_rows=cast_rows),
        grid=(n_tile + 1,),
        in_specs=[pl.BlockSpec((tm, d), last),
                  pl.BlockSpec((ns, d), lambda i: (0, 0)),
                  pl.BlockSpec((1, d), lambda i: (0, 0)),
                  pl.BlockSpec(memory_space=pl.ANY)],
        out_specs=[pl.BlockSpec((tm, width), last),
                   pl.BlockSpec((ns, width), lambda i: (0, 0))],
        out_shape=[jax.ShapeDtypeStruct((n, width), BF16),
                   jax.ShapeDtypeStruct((ns, width), BF16)],
        scratch_shapes=[pltpu.VMEM((d, width), BF16),
                        pltpu.VMEM((2, cast_rows, width), F32),
                        pltpu.SemaphoreType.DMA((2,))],
        compiler_params=pltpu.CompilerParams(
            dimension_semantics=("arbitrary",), vmem_limit_bytes=VMEM_LIMIT),
        name="inproj",
    )(xp2, xs2, gain, w_f32)


def _s5_body(u_ref, h0_ref, lam_ref, bmat_ref, cmat_ref, d_ref, y_ref, hout_ref,
             utb_ref, ytb_ref, bu_ref, hst_ref, *, nb, tl, n_slab, n_state, n_sub):
    sub_t = tl // n_sub
    sub_m = sub_t * nb
    n_pass = n_state // SCAN_LANES

    @pl.when(pl.program_id(0) == 0)
    def _():
        hst_ref[...] = h0_ref[...]

    for s in range(n_slab):
        for b in range(nb):
            utb_ref[s, pl.ds(b, tl, stride=nb), :] = u_ref[b, :, s * LANES:(s + 1) * LANES].astype(F32)

    def re_cols(s, width=SLAB_STATES):
        return slice(s * width, (s + 1) * width)

    def im_cols(s, width=SLAB_STATES):
        return slice(n_state + s * width, n_state + (s + 1) * width)

    def project_in(k):
        rows = slice(k * sub_m, (k + 1) * sub_m)
        for s in range(n_slab):
            bu = jnp.dot(utb_ref[s, rows, :].astype(BF16), bmat_ref[s], preferred_element_type=F32)
            bu_ref[rows, re_cols(s)] = bu[:, :SLAB_STATES]
            bu_ref[rows, im_cols(s)] = bu[:, SLAB_STATES:]

    def scan(k, state):
        out = []
        for c, (hr, hi) in enumerate(state):
            re, im = re_cols(c, SCAN_LANES), im_cols(c, SCAN_LANES)
            lr = lam_ref[0, :, re]
            li = lam_ref[1, :, re]
            for t in range(k * sub_t, (k + 1) * sub_t):
                r = slice(t * nb, (t + 1) * nb)
                hr, hi = lr * hr - li * hi + bu_ref[r, re], lr * hi + li * hr + bu_ref[r, im]
                bu_ref[r, re] = hr
                bu_ref[r, im] = hi
            out.append((hr, hi))
        return out

    def project_out(k):
        rows = slice(k * sub_m, (k + 1) * sub_m)
        for s in range(n_slab):
            y = (jnp.dot(bu_ref[rows, re_cols(s)].astype(BF16), cmat_ref[s, :SLAB_STATES, :], preferred_element_type=F32)
                 + jnp.dot(bu_ref[rows, im_cols(s)].astype(BF16), cmat_ref[s, SLAB_STATES:, :], preferred_element_type=F32))
            ytb_ref[s, rows, :] = y + d_ref[s] * utb_ref[s, rows, :]

    state = [(hst_ref[:, re_cols(c, SCAN_LANES)], hst_ref[:, im_cols(c, SCAN_LANES)]) for c in range(n_pass)]
    project_in(0)
    for k in range(n_sub):
        if k + 1 < n_sub:
            project_in(k + 1)
        state = scan(k, state)
        if k >= 1:
            project_out(k - 1)
    project_out(n_sub - 1)
    for c, (hr, hi) in enumerate(state):
        hst_ref[:, re_cols(c, SCAN_LANES)] = hr
        hst_ref[:, im_cols(c, SCAN_LANES)] = hi
    hout_ref[...] = hst_ref[...]

    for s in range(n_slab):
        for b in range(nb):
            y_ref[b, :, s * LANES:(s + 1) * LANES] = ytb_ref[s, pl.ds(b, tl, stride=nb), :].astype(BF16)


def _s5(h3, h0, lam_b, bmat, cmat, d_slab, tl, n_sub):
    nb, seq, _ = h3.shape
    n_slab = bmat.shape[0]
    width = n_slab * LANES
    n_state = n_slab * SLAB_STATES
    m = tl * nb
    body = functools.partial(_s5_body, nb=nb, tl=tl, n_slab=n_slab, n_state=n_state, n_sub=n_sub)
    const = lambda *shape: pl.BlockSpec(shape, lambda i: (0,) * len(shape))
    return pl.pallas_call(
        body,
        grid=(seq // tl,),
        in_specs=[pl.BlockSpec((nb, tl, width), lambda i: (0, i, 0)),
                  const(nb, 2 * n_state),
                  const(2, nb, n_state),
                  const(n_slab, LANES, 2 * SLAB_STATES),
                  const(n_slab, 2 * SLAB_STATES, LANES),
                  const(n_slab, 1, LANES)],
        out_specs=[pl.BlockSpec((nb, tl, width), lambda i: (0, i, 0)),
                   const(nb, 2 * n_state)],
        out_shape=[jax.ShapeDtypeStruct((nb, seq, width), BF16),
                   jax.ShapeDtypeStruct((nb, 2 * n_state), F32)],
        scratch_shapes=[pltpu.VMEM((n_slab, m, LANES), F32),
                        pltpu.VMEM((n_slab, m, LANES), F32),
                        pltpu.VMEM((m, 2 * n_state), F32),
                        pltpu.VMEM((nb, 2 * n_state), F32)],
        compiler_params=pltpu.CompilerParams(
            dimension_semantics=("arbitrary",), vmem_limit_bytes=VMEM_LIMIT),
        name="s5",
    )(h3, h0, lam_b, bmat, cmat, d_slab)


def _lo_lanes():
    return lax.broadcasted_iota(jnp.int32, (1, LANES), 1) < HEAD_DIM


def _head_norm(t, gain2_scaled):
    lo = _lo_lanes()
    sq = t * t
    ss_lo = jnp.sum(jnp.where(lo, sq, 0.0), axis=-1, keepdims=True)
    ss_hi = jnp.sum(jnp.where(lo, 0.0, sq), axis=-1, keepdims=True)
    return t * lax.rsqrt(jnp.where(lo, ss_lo, ss_hi) + HEAD_DIM * EPS) * gain2_scaled


def _pair_blockdiag(t):
    lo = _lo_lanes()
    r = pltpu.roll(t, HEAD_DIM, 1)
    return ((jnp.where(lo, t, 0.0), jnp.where(lo, 0.0, r)),
            (jnp.where(lo, r, 0.0), jnp.where(lo, 0.0, t)))


def _rowmax(parts):
    full, out = None, None
    for p in parts:
        w = p.shape[1]
        for c0 in range(0, w - w % LANES, LANES):
            blk = p[:, c0:c0 + LANES]
            full = blk if full is None else jnp.maximum(full, blk)
        if w % LANES:
            r = jnp.max(p[:, w - w % LANES:], axis=-1, keepdims=True)
            out = r if out is None else jnp.maximum(out, r)
    if full is not None:
        r = jnp.max(full, axis=-1, keepdims=True)
        out = r if out is None else jnp.maximum(out, r)
    return out


def _softmax_numerators(sc, sinks):
    s = sc.shape[1] // 2
    b = (s // LANES) * LANES
    left, mid, right = sc[:, :b], sc[:, b:b + LANES], sc[:, b + LANES:]
    in0 = lax.broadcasted_iota(jnp.int32, (1, LANES), 1) < (s - b)
    m0 = _rowmax([left, jnp.where(in0, mid, sinks[0])])
    m1 = _rowmax([jnp.where(in0, sinks[1], mid), right])
    e = [jnp.exp2(left[:, c:c + LANES] - m0) for c in range(0, b, LANES)]
    e.append(jnp.exp2(mid - jnp.where(in0, m0, m1)))
    e += [jnp.exp2(right[:, c:c + LANES] - m1) for c in range(0, right.shape[1], LANES)]
    sink_share = jnp.exp2(sinks[2] - jnp.where(_lo_lanes(), m0, m1))
    return jnp.concatenate(e, axis=1).astype(BF16), sink_share


def _scores(qn, kbd, bias):
    return lax.dot_general(qn, kbd, (((1,), (1,)), ((), ())), preferred_element_type=F32) + bias


def _weighted_values(e, vbd1, sink_share):
    od = jnp.dot(e, vbd1, preferred_element_type=F32)
    return od[:, :LANES] / (od[:, LANES:] + sink_share)


def _band_attn_body(q_ref, k_ref, v_ref, qg_ref, kg_ref, bias_ref, sink_ref, ones_ref, o_ref, pk_ref, pv_ref,
                    kvbd_ref, qn_ref, sc0_ref, sc1_ref, e0_ref, e1_ref, share0_ref, share1_ref, *, seq, qb):
    j = pl.program_id(1)
    pad = BAND_CHUNKS * CHUNK
    span = pad + CHUNK
    rows = min(512, seq)
    n_pair = GQA_GROUP // 2
    n_chunk = qb // CHUNK
    sc_refs, e_refs, share_refs = (sc0_ref, sc1_ref), (e0_ref, e1_ref), (share0_ref, share1_ref)
    assert n_chunk >= 4 and n_chunk % 2 == 0

    @pl.when(j == 0)
    def _():
        zeros = jnp.zeros((pad, LANES), BF16)
        for a in range(8):
            kvbd_ref[a, 0:pad, :] = zeros

        def fill(i, carry):
            r0 = pl.multiple_of(i * rows, rows)
            kn = _head_norm(k_ref[0, pl.ds(r0, rows), :].astype(F32), kg_ref[...])
            for is_v, t in enumerate((kn, v_ref[0, pl.ds(r0, rows), :].astype(F32))):
                for kv, halves in enumerate(_pair_blockdiag(t)):
                    for tb, half in enumerate(halves):
                        kvbd_ref[4 * is_v + 2 * kv + tb, pl.ds(pad + r0, rows), :] = half.astype(BF16)
            return carry

        lax.fori_loop(0, seq // rows, fill, 0)
        pk_ref[0] = _head_norm(k_ref[0, seq - WINDOW:seq, :].astype(F32), kg_ref[...])
        pv_ref[0] = v_ref[0, seq - WINDOW:seq, :].astype(F32)

    def lanes_of(kv, i):
        return slice((kv * n_pair + i) * LANES, (kv * n_pair + i + 1) * LANES)

    def stage_norm(c):
        q0 = pl.multiple_of(c * CHUNK, CHUNK)
        for kv in range(2):
            for i in range(n_pair):
                qn_ref[c, kv, i * CHUNK:(i + 1) * CHUNK, :] = _head_norm(
                    q_ref[0, pl.ds(q0, CHUNK), lanes_of(kv, i)].astype(F32), qg_ref[...]).astype(BF16)

    def key_rows(c):
        return pl.ds(pl.multiple_of((j * n_chunk + c) * CHUNK, CHUNK), span)

    def stage_scores(c, slot):
        variant = jnp.minimum(j * n_chunk + c, BAND_CHUNKS)
        for kv in range(2):
            kbd = jnp.concatenate([kvbd_ref[2 * kv, key_rows(c), :], kvbd_ref[2 * kv + 1, key_rows(c), :]], axis=0)
            sc_refs[slot][kv] = _scores(qn_ref[c, kv], kbd, bias_ref[variant, kv])

    def stage_exp(slot):
        for kv in range(2):
            e_refs[slot][kv], share_refs[slot][kv] = _softmax_numerators(sc_refs[slot][kv], sink_ref[kv])

    def stage_values(c, slot):
        q0 = pl.multiple_of(c * CHUNK, CHUNK)
        for kv in range(2):
            vbd = jnp.concatenate([kvbd_ref[4 + 2 * kv, key_rows(c), :], kvbd_ref[4 + 2 * kv + 1, key_rows(c), :]], axis=0)
            o = _weighted_values(e_refs[slot][kv], jnp.concatenate([vbd, ones_ref[...]], axis=1), share_refs[slot][kv])
            for i in range(n_pair):
                o_ref[0, pl.ds(q0, CHUNK), lanes_of(kv, i)] = o[i * CHUNK:(i + 1) * CHUNK].astype(BF16)

    def iteration(i, parity):
        live = (lambda c: 0 <= c < n_chunk) if isinstance(i, int) else (lambda c: True)
        if live(i - 3):
            stage_values(i - 3, 1 - parity)
        if live(i - 2):
            stage_exp(parity)
        if live(i - 1):
            stage_scores(i - 1, 1 - parity)
        if live(i):
            stage_norm(i)

    for i in range(4):
        iteration(i, i % 2)

    def steady(p, carry):
        iteration(4 + 2 * p, 0)
        iteration(5 + 2 * p, 1)
        return carry

    lax.fori_loop(0, (n_chunk - 4) // 2, steady, 0)
    for i in range(n_chunk, n_chunk + 3):
        iteration(i, i % 2)


def _band_attn(h3, qcol, kcol, vcol, qgain2, kgain2, bias, sinkcol, ones_bd, qb):
    nb, seq, _ = h3.shape
    aw = 2 * GQA_GROUP * HEAD_DIM
    pad = BAND_CHUNKS * CHUNK
    body = functools.partial(_band_attn_body, seq=seq, qb=qb)
    const = lambda *shape: pl.BlockSpec(shape, lambda b, j: (0,) * len(shape))
    return pl.pallas_call(
        body,
        grid=(nb, seq // qb),
        in_specs=[pl.BlockSpec((1, qb, aw), lambda b, j: (b, j, qcol // aw)),
                  pl.BlockSpec((1, seq, LANES), lambda b, j: (b, 0, kcol // LANES)),
                  pl.BlockSpec((1, seq, LANES), lambda b, j: (b, 0, vcol // LANES)),
                  const(1, LANES), const(1, LANES),
                  const(*bias.shape), const(*sinkcol.shape), const(*ones_bd.shape)],
        out_specs=[pl.BlockSpec((1, qb, aw), lambda b, j: (b, j, 0)),
                   pl.BlockSpec((1, WINDOW, LANES), lambda b, j: (b, 0, 0)),
                   pl.BlockSpec((1, WINDOW, LANES), lambda b, j: (b, 0, 0))],
        out_shape=[jax.ShapeDtypeStruct((nb, seq, aw), BF16),
                   jax.ShapeDtypeStruct((nb, WINDOW, LANES), F32),
                   jax.ShapeDtypeStruct((nb, WINDOW, LANES), F32)],
        scratch_shapes=[pltpu.VMEM((8, seq + pad, LANES), BF16),
                        pltpu.VMEM((qb // CHUNK, 2, (GQA_GROUP // 2) * CHUNK, LANES), BF16),
                        *[pltpu.VMEM((2, (GQA_GROUP // 2) * CHUNK, 2 * (pad + CHUNK)), F32)] * 2,
                        *[pltpu.VMEM((2, (GQA_GROUP // 2) * CHUNK, 2 * (pad + CHUNK)), BF16)] * 2,
                        *[pltpu.VMEM((2, (GQA_GROUP // 2) * CHUNK, LANES), F32)] * 2],
        compiler_params=pltpu.CompilerParams(
            dimension_semantics=("parallel", "arbitrary"), vmem_limit_bytes=VMEM_LIMIT),
        name="band_attn",
    )(h3, h3, h3, qgain2, kgain2, bias, sinkcol, ones_bd)


def _cached_attn_body(q_ref, k_ref, v_ref, ck_ref, cv_ref, qg_ref, kg_ref, bias_ref, sink_ref, ones_ref,
                      o_ref, sk_ref, sv_ref):
    t = q_ref.shape[1]
    n_pair = GQA_GROUP // 2
    kn = _head_norm(k_ref[0].astype(F32), kg_ref[...])
    vn = v_ref[0].astype(F32)
    sk_ref[0] = kn
    sv_ref[0] = vn
    kbds = _pair_blockdiag(jnp.concatenate([ck_ref[0], kn], axis=0))
    vbds = _pair_blockdiag(jnp.concatenate([cv_ref[0], vn], axis=0))
    for kv in range(2):
        qn = jnp.concatenate(
            [_head_norm(q_ref[0, :, (kv * n_pair + i) * LANES:(kv * n_pair + i + 1) * LANES].astype(F32), qg_ref[...])
             for i in range(n_pair)], axis=0).astype(BF16)
        kbd = jnp.concatenate(kbds[kv], axis=0).astype(BF16)
        vbd1 = jnp.concatenate([jnp.concatenate(vbds[kv], axis=0).astype(BF16), ones_ref[...]], axis=1)
        e, sink_share = _softmax_numerators(_scores(qn, kbd, bias_ref[0, kv]), sink_ref[kv])
        o = _weighted_values(e, vbd1, sink_share)
        for i in range(n_pair):
            o_ref[0, :, (kv * n_pair + i) * LANES:(kv * n_pair + i + 1) * LANES] = o[i * t:(i + 1) * t].astype(BF16)


def _cached_attn(h3, qcol, kcol, vcol, cache_k, cache_v, qgain2, kgain2, bias, sinkcol, ones_bd):
    nb, t, _ = h3.shape
    aw = 2 * GQA_GROUP * HEAD_DIM
    r = cache_k.shape[1]
    const = lambda *shape: pl.BlockSpec(shape, lambda b: (0,) * len(shape))
    return pl.pallas_call(
        _cached_attn_body,
        grid=(nb,),
        in_specs=[pl.BlockSpec((1, t, aw), lambda b: (b, 0, qcol // aw)),
                  pl.BlockSpec((1, t, LANES), lambda b: (b, 0, kcol // LANES)),
                  pl.BlockSpec((1, t, LANES), lambda b: (b, 0, vcol // LANES)),
                  pl.BlockSpec((1, r, LANES), lambda b: (b, 0, 0)),
                  pl.BlockSpec((1, r, LANES), lambda b: (b, 0, 0)),
                  const(1, LANES), const(1, LANES),
                  const(*bias.shape), const(*sinkcol.shape), const(*ones_bd.shape)],
        out_specs=[pl.BlockSpec((1, t, aw), lambda b: (b, 0, 0)),
                   pl.BlockSpec((1, t, LANES), lambda b: (b, 0, 0)),
                   pl.BlockSpec((1, t, LANES), lambda b: (b, 0, 0))],
        out_shape=[jax.ShapeDtypeStruct((nb, t, aw), BF16),
                   jax.ShapeDtypeStruct((nb, t, LANES), F32),
                   jax.ShapeDtypeStruct((nb, t, LANES), F32)],
        compiler_params=pltpu.CompilerParams(
            dimension_semantics=("parallel",), vmem_limit_bytes=VMEM_LIMIT),
        name="cached_attn",
    )(h3, h3, h3, cache_k, cache_v, qgain2, kgain2, bias, sinkcol, ones_bd)


def _sigmoid(x):
    return 1.0 / (1.0 + jnp.exp2(x * -LOG2E))


def _merge_body(x_ref, y_ref, za_ref, o_ref, zb_ref, ga_ref, gb_ref,
                wglu_ref, bglu_ref, woa_ref, wob_ref, wo_ref, out_ref, *, n_part):
    part = x_ref.shape[0] // n_part
    for p in range(n_part):
        r = slice(p * part, (p + 1) * part)
        y = y_ref[r, :].astype(F32)
        g = 0.5 * y * (1.0 + lax.erf(y * (2.0 ** -0.5)))
        gl = jnp.dot(g.astype(BF16), wglu_ref[...], preferred_element_type=F32) + bglu_ref[...]
        za = za_ref[r, :].astype(F32)
        br_a = g * _sigmoid(gl) * (za * _sigmoid(za))
        zb = zb_ref[r, :].astype(F32)
        br_b = o_ref[r, :].astype(F32) * (zb * _sigmoid(zb))
        mixed = (_sigmoid(ga_ref[r, :].astype(F32)) * jnp.dot(br_a.astype(BF16), woa_ref[...], preferred_element_type=F32)
                 + _sigmoid(gb_ref[r, :].astype(F32)) * jnp.dot(br_b.astype(BF16), wob_ref[...], preferred_element_type=F32))
        out_ref[r, :] = x_ref[r, :] + jnp.dot(mixed.astype(BF16), wo_ref[...], preferred_element_type=F32)


def _merge(x2, y2, o2, h2, cols, wglu, bglu, woa, wob, wo, tm, n_part):
    n, d = x2.shape
    sw = y2.shape[1]
    aw = o2.shape[1]
    za_col, zb_col, ga_col, gb_col = cols
    row = lambda width, col: pl.BlockSpec((pl.Element(tm), pl.Element(width)), lambda i: (i * tm, col))
    weight = lambda a: pl.BlockSpec(a.shape, lambda i: (0, 0), pipeline_mode=pl.Buffered(1))
    return pl.pallas_call(
        functools.partial(_merge_body, n_part=n_part),
        grid=(n // tm,),
        in_specs=[row(d, 0), row(sw, 0), row(sw, za_col), row(aw, 0), row(aw, zb_col),
                  row(d, ga_col), row(d, gb_col),
                  weight(wglu), weight(bglu), weight(woa), weight(wob), weight(wo)],
        out_specs=row(d, 0),
        out_shape=jax.ShapeDtypeStruct((n, d), F32),
        compiler_params=pltpu.CompilerParams(
            dimension_semantics=("parallel",), vmem_limit_bytes=VMEM_LIMIT),
        name="merge",
    )(x2, y2, h2, o2, h2, h2, h2, wglu, bglu, woa, wob, wo)


def _discretize(a_re, a_im, log_dt, b_re, b_im):
    dt = jnp.exp(log_dt)[:, None]
    mag = jnp.exp(a_re * dt)
    ang = a_im * dt
    lam_re = mag * jnp.cos(ang)
    lam_im = mag * jnp.sin(ang)
    den = a_re * a_re + a_im * a_im
    cr = ((lam_re - 1.0) * a_re + lam_im * a_im) / den
    ci = (lam_im * a_re - (lam_re - 1.0) * a_im) / den
    bb_re = cr[..., None] * b_re - ci[..., None] * b_im
    bb_im = cr[..., None] * b_im + ci[..., None] * b_re
    return lam_re, lam_im, bb_re, bb_im


def _slab_blockdiag(t):
    n_slab, g, a, b = t.shape
    same_group = jnp.eye(g, dtype=bool)[None, :, None, :, None]
    return jnp.where(same_group, t[:, :, :, None, :], 0.0).reshape(n_slab, g * a, g * b)


def _t5_bucket(rel):
    half = N_BUCKETS // 2
    n = -rel
    ret = jnp.where(n < 0, half, 0)
    n = jnp.abs(n)
    max_exact = half // 2
    nf = jnp.maximum(n, 1).astype(F32)
    large = max_exact + (jnp.log(nf / max_exact) / math.log(MAX_DISTANCE / max_exact)
                         * (half - max_exact)).astype(jnp.int32)
    large = jnp.minimum(large, half - 1)
    return ret + jnp.where(n < max_exact, n, large)


def _pair_bias(rel, table):
    t, s = rel.shape
    onehot = (_t5_bucket(rel)[..., None] == jnp.arange(N_BUCKETS)).astype(F32)
    b = jnp.einsum("tsn,nh->tsh", onehot, table.astype(F32) * LOG2E, precision=lax.Precision.HIGHEST)
    b = b.reshape(t, s, 2, GQA_GROUP // 2, 2)
    return jnp.transpose(b, (2, 3, 0, 4, 1)).reshape(2, (GQA_GROUP // 2) * t, 2 * s)


def _pair_sinks(sinks, t):
    sk = jnp.transpose(sinks.astype(F32).reshape(2, GQA_GROUP // 2, 2), (0, 2, 1))[:, :, :, None, None]
    sk = jnp.broadcast_to(sk, (2, 2, GQA_GROUP // 2, t, LANES)).reshape(2, 2, (GQA_GROUP // 2) * t, LANES)
    lo = jnp.arange(LANES) < HEAD_DIM
    return jnp.concatenate([sk, jnp.where(lo, sk[:, 0], sk[:, 1])[:, None]], axis=1) * LOG2E


def _pair_ones(s):
    first = (jnp.arange(2 * s) < s)[:, None]
    lo = (jnp.arange(LANES) < HEAD_DIM)[None, :]
    return (first == lo).astype(BF16)


def kernel(x_prompt, x_sample, cache_k, cache_v, state_ssm_re, state_ssm_im, norm_gain, w_in, ssm_a_re, ssm_a_im, ssm_log_dt, ssm_b_re, ssm_b_im, ssm_c_re, ssm_c_im, ssm_d, w_glu, b_glu, q_gain, k_gain, attn_sinks, rel_bias, w_out_a, w_out_b, w_o):
    nb, seq, d = x_prompt.shape
    db, dseq, _ = x_sample.shape
    n_groups, n_st = ssm_a_re.shape[1:]
    sw = n_groups * SSM_GROUP
    aw = 2 * GQA_GROUP * HEAD_DIM
    kvw = 2 * HEAD_DIM
    n_slab = sw // LANES
    n_state = n_groups * n_st
    l = 0

    c_u, c_za, c_q = 0, sw, 2 * sw
    c_k = c_q + aw
    c_v = c_k + kvw
    c_zb = c_v + kvw
    c_ga = c_zb + aw
    c_gb = c_ga + d
    gain = norm_gain[l].astype(F32).reshape(1, d)

    lam_re, lam_im, bb_re, bb_im = _discretize(
        ssm_a_re[l].astype(F32), ssm_a_im[l].astype(F32), ssm_log_dt[l].astype(F32),
        ssm_b_re[l].astype(F32), ssm_b_im[l].astype(F32))
    lam_b = jnp.broadcast_to(jnp.stack([lam_re.reshape(-1), lam_im.reshape(-1)])[:, None, :], (2, nb, n_state))
    slab = lambda t: t.reshape(n_slab, SLAB_GROUPS, *t.shape[1:])
    bmat = jnp.concatenate([_slab_blockdiag(jnp.swapaxes(slab(bb_re), 2, 3)),
                            _slab_blockdiag(jnp.swapaxes(slab(bb_im), 2, 3))], axis=2).astype(BF16)
    cmat = jnp.concatenate([_slab_blockdiag(jnp.swapaxes(slab(ssm_c_re[l].astype(F32)), 2, 3)),
                            _slab_blockdiag(jnp.swapaxes(slab(-ssm_c_im[l].astype(F32)), 2, 3))], axis=1).astype(BF16)
    d_slab = ssm_d[l].astype(F32).reshape(n_slab, 1, LANES)

    qgain2 = jnp.tile(q_gain[l].astype(F32) * (HEAD_DIM ** 0.5 * SCALE * LOG2E), 2).reshape(1, LANES)
    kgain2 = jnp.tile(k_gain[l].astype(F32) * HEAD_DIM ** 0.5, 2).reshape(1, LANES)
    span = (BAND_CHUNKS + 1) * CHUNK
    rel_p = jnp.arange(span)[None, :] - BAND_CHUNKS * CHUNK - jnp.arange(CHUNK)[:, None]
    bias_p = _pair_bias(rel_p, rel_bias)
    key_off = jnp.tile(jnp.arange(span), 2)[None, None, None, :]
    first_valid = ((BAND_CHUNKS - jnp.arange(BAND_CHUNKS + 1)) * CHUNK)[:, None, None, None]
    bias_p = jnp.where(key_off >= first_valid, bias_p[None], -jnp.inf)
    sink_p = _pair_sinks(attn_sinks[l], CHUNK)
    rows = cache_k.shape[2]
    rel_s = jnp.arange(rows + dseq)[None, :] - rows - jnp.arange(dseq)[:, None]
    bias_s = _pair_bias(rel_s, rel_bias)[None]
    sink_s = _pair_sinks(attn_sinks[l], dseq)

    wglu = w_glu[l].astype(BF16)
    bglu = b_glu[l].astype(F32).reshape(1, sw)
    woa = w_out_a[l].astype(BF16)
    wob = w_out_b[l].astype(BF16)
    wo = w_o[l].astype(BF16)
    merge_cols = (c_za, c_zb, c_ga, c_gb)

    xp2 = x_prompt.reshape(nb * seq, d)
    xs2 = x_sample.reshape(db * dseq, d)
    hp, hs = _inproj(xp2, xs2, gain, w_in[l], tm=min(256, nb * seq), cast_rows=64)
    hp3 = hp.reshape(nb, seq, -1)
    yp, hfin_p = _s5(hp3, jnp.zeros((nb, 2 * n_state), F32), lam_b, bmat, cmat, d_slab, tl=min(64, seq), n_sub=2)
    op, pk, pv = _band_attn(hp3, c_q, c_k, c_v, qgain2, kgain2, bias_p, sink_p, _pair_ones(span), qb=min(2048, seq))
    y_p = _merge(xp2, yp.reshape(nb * seq, sw), op.reshape(nb * seq, aw), hp, merge_cols,
                 wglu, bglu, woa, wob, wo, tm=512, n_part=2).reshape(nb, seq, d)

    hs3 = hs.reshape(db, dseq, -1)
    h0 = jnp.concatenate([state_ssm_re[l].reshape(db, n_state), state_ssm_im[l].reshape(db, n_state)],
                         axis=1).astype(F32)
    ys, hfin_s = _s5(hs3, h0, lam_b, bmat, cmat, d_slab, tl=dseq, n_sub=1)
    ck = cache_k[l].reshape(db, rows, kvw).astype(F32)
    cv = cache_v[l].reshape(db, rows, kvw).astype(F32)
    os_, sk, sv = _cached_attn(hs3, c_q, c_k, c_v, ck, cv, qgain2, kgain2, bias_s, sink_s, _pair_ones(rows + dseq))
    y_s = _merge(xs2, ys.reshape(db * dseq, sw), os_.reshape(db * dseq, aw), hs, merge_cols,
                 wglu, bglu, woa, wob, wo, tm=db * dseq, n_part=1).reshape(db, dseq, d)

    st = lambda h, nbb: (h[:, :n_state].reshape(1, nbb, n_groups, n_st), h[:, n_state:].reshape(1, nbb, n_groups, n_st))
    p_re, p_im = st(hfin_p, nb)
    s_re, s_im = st(hfin_s, db)
    kvshape = lambda a: a.reshape(1, a.shape[0], a.shape[1], 2, HEAD_DIM)
    return (y_p, y_s, p_re, p_im, kvshape(pk), kvshape(pv), s_re, s_im, kvshape(sk), kvshape(sv))
```

```python
import functools
import math
from typing import NamedTuple

import jax
import jax.numpy as jnp
from jax import lax
from jax.experimental import pallas as pl
from jax.experimental.pallas import tpu as pltpu

F32 = jnp.float32
BF16 = jnp.bfloat16

LANES = 128
SUBLANES = 8
VMEM_LIMIT = 56 * 1024 * 1024

CHUNK = 64
HEAD_DIM = 64
GQA_GROUP = 8
SSM_GROUP = 16
SSM_STATE = 64
WINDOW = 128
BAND_CHUNKS = 2
N_BUCKETS = 32
MAX_DISTANCE = 128
EPS = 1e-6
SCALE = HEAD_DIM ** -0.5
LOG2E = math.log2(math.e)
SLAB_GROUPS = LANES // SSM_GROUP
SLAB_STATES = SLAB_GROUPS * SSM_STATE
SCAN_LANES = 1024


def _inproj_body(xp_ref, xs_ref, g_ref, w_hbm, hp_ref, hs_ref, w_ref, stage_ref, sem, *, n_tile, cast_rows):
    i = pl.program_id(0)
    d = w_ref.shape[0]
    n_chunk = d // cast_rows

    def chunk_copy(c, slot):
        return pltpu.make_async_copy(w_hbm.at[pl.ds(c * cast_rows, cast_rows), :], stage_ref.at[slot], sem.at[slot])

    @pl.when(i == 0)
    def _():
        chunk_copy(0, 0).start()

        def cast(c, carry):
            slot = c % 2

            @pl.when(c + 1 < n_chunk)
            def _():
                chunk_copy(c + 1, 1 - slot).start()

            chunk_copy(c, slot).wait()
            w_ref[pl.ds(pl.multiple_of(c * cast_rows, cast_rows), cast_rows), :] = stage_ref[slot].astype(BF16)
            return carry

        lax.fori_loop(0, n_chunk, cast, 0)

    def project(x_ref, o_ref):
        x = x_ref[...]
        r = lax.rsqrt(jnp.mean(x * x, axis=-1, keepdims=True) + EPS)
        xg = (x * g_ref[...]).astype(BF16)
        o_ref[...] = (jnp.dot(xg, w_ref[...], preferred_element_type=F32) * r).astype(BF16)

    @pl.when(i < n_tile)
    def _():
        project(xp_ref, hp_ref)

    @pl.when(i == n_tile)
    def _():
        project(xs_ref, hs_ref)


def _inproj(xp2, xs2, gain, w_f32, tm, cast_rows):
    n, d = xp2.shape
    ns = xs2.shape[0]
    width = w_f32.shape[1]
    n_tile = n // tm
    last = lambda i: (jnp.minimum(i, n_tile - 1), 0)
    return pl.pallas_call(
        functools.partial(_inproj_body, n_tile=n_tile, cast_rows=cast_rows),
        grid=(n_tile + 1,),
        in_specs=[pl.BlockSpec((tm, d), last),
                  pl.BlockSpec((ns, d), lambda i: (0, 0)),
                  pl.BlockSpec((1, d), lambda i: (0, 0)),
                  pl.BlockSpec(memory_space=pl.ANY)],
        out_specs=[pl.BlockSpec((tm, width), last),
                   pl.BlockSpec((ns, width), lambda i: (0, 0))],
        out_shape=[jax.ShapeDtypeStruct((n, width), BF16),
                   jax.ShapeDtypeStruct((ns, width), BF16)],
        scratch_shapes=[pltpu.VMEM((d, width), BF16),
                        pltpu.VMEM((2, cast_rows, width), F32),
                        pltpu.SemaphoreType.DMA((2,))],
        compiler_params=pltpu.CompilerParams(
            dimension_semantics=("arbitrary",), vmem_limit_bytes=VMEM_LIMIT),
        name="inproj",
    )(xp2, xs2, gain, w_f32)


def _s5_body(u_ref, h0_ref, lam_ref, bmat_ref, cmat_ref, d_ref, y_ref, hout_ref,
             utb_ref, ytb_ref, bu_ref, hst_ref, *, nb, tl, n_slab, n_state, n_sub):
    sub_t = tl // n_sub
    sub_m = sub_t * nb
    n_pass = n_state // SCAN_LANES

    @pl.when(pl.program_id(0) == 0)
    def _():
        hst_ref[...] = h0_ref[...]

    for s in range(n_slab):
        for b in range(nb):
            utb_ref[s, pl.ds(b, tl, stride=nb), :] = u_ref[b, :, s * LANES:(s + 1) * LANES].astype(F32)

    def re_cols(s, width=SLAB_STATES):
        return slice(s * width, (s + 1) * width)

    def im_cols(s, width=SLAB_STATES):
        return slice(n_state + s * width, n_state + (s + 1) * width)

    def project_in(k):
        rows = slice(k * sub_m, (k + 1) * sub_m)
        for s in range(n_slab):
            bu = jnp.dot(utb_ref[s, rows, :].astype(BF16), bmat_ref[s], preferred_element_type=F32)
            bu_ref[rows, re_cols(s)] = bu[:, :SLAB_STATES]
            bu_ref[rows, im_cols(s)] = bu[:, SLAB_STATES:]

    def scan(k, state):
        out = []
        for c, (hr, hi) in enumerate(state):
            re, im = re_cols(c, SCAN_LANES), im_cols(c, SCAN_LANES)
            lr = lam_ref[0, :, re]
            li = lam_ref[1, :, re]
            for t in range(k * sub_t, (k + 1) * sub_t):
                r = slice(t * nb, (t + 1) * nb)
                hr, hi = lr * hr - li * hi + bu_ref[r, re], lr * hi + li * hr + bu_ref[r, im]
                bu_ref[r, re] = hr
                bu_ref[r, im] = hi
            out.append((hr, hi))
        return out

    def project_out(k):
        rows = slice(k * sub_m, (k + 1) * sub_m)
        for s in range(n_slab):
            y = (jnp.dot(bu_ref[rows, re_cols(s)].astype(BF16), cmat_ref[s, :SLAB_STATES, :], preferred_element_type=F32)
                 + jnp.dot(bu_ref[rows, im_cols(s)].astype(BF16), cmat_ref[s, SLAB_STATES:, :], preferred_element_type=F32))
            ytb_ref[s, rows, :] = y + d_ref[s] * utb_ref[s, rows, :]

    state = [(hst_ref[:, re_cols(c, SCAN_LANES)], hst_ref[:, im_cols(c, SCAN_LANES)]) for c in range(n_pass)]
    project_in(0)
    for k in range(n_sub):
        if k + 1 < n_sub:
            project_in(k + 1)
        state = scan(k, state)
        if k >= 1:
            project_out(k - 1)
    project_out(n_sub - 1)
    for c, (hr, hi) in enumerate(state):
        hst_ref[:, re_cols(c, SCAN_LANES)] = hr
        hst_ref[:, im_cols(c, SCAN_LANES)] = hi
    hout_ref[...] = hst_ref[...]

    for s in range(n_slab):
        for b in range(nb):
            y_ref[b, :, s * LANES:(s + 1) * LANES] = ytb_ref[s, pl.ds(b, tl, stride=nb), :].astype(BF16)


def _s5(h3, h0, lam_b, bmat, cmat, d_slab, tl, n_sub):
    nb, seq, _ = h3.shape
    n_slab = bmat.shape[0]
    width = n_slab * LANES
    n_state = n_slab * SLAB_STATES
    m = tl * nb
    body = functools.partial(_s5_body, nb=nb, tl=tl, n_slab=n_slab, n_state=n_state, n_sub=n_sub)
    const = lambda *shape: pl.BlockSpec(shape, lambda i: (0,) * len(shape))
    return pl.pallas_call(
        body,
        grid=(seq // tl,),
        in_specs=[pl.BlockSpec((nb, tl, width), lambda i: (0, i, 0)),
                  const(nb, 2 * n_state),
                  const(2, nb, n_state),
                  const(n_slab, LANES, 2 * SLAB_STATES),
                  const(n_slab, 2 * SLAB_STATES, LANES),
                  const(n_slab, 1, LANES)],
        out_specs=[pl.BlockSpec((nb, tl, width), lambda i: (0, i, 0)),
                   const(nb, 2 * n_state)],
        out_shape=[jax.ShapeDtypeStruct((nb, seq, width), BF16),
                   jax.ShapeDtypeStruct((nb, 2 * n_state), F32)],
        scratch_shapes=[pltpu.VMEM((n_slab, m, LANES), F32),
                        pltpu.VMEM((n_slab, m, LANES), F32),
                        pltpu.VMEM((m, 2 * n_state), F32),
                        pltpu.VMEM((nb, 2 * n_state), F32)],
        compiler_params=pltpu.CompilerParams(
            dimension_semantics=("arbitrary",), vmem_limit_bytes=VMEM_LIMIT),
        name="s5",
    )(h3, h0, lam_b, bmat, cmat, d_slab)


def _lo_lanes():
    return lax.broadcasted_iota(jnp.int32, (1, LANES), 1) < HEAD_DIM


def _head_norm(t, gain2_scaled):
    lo = _lo_lanes()
    sq = t * t
    ss_lo = jnp.sum(jnp.where(lo, sq, 0.0), axis=-1, keepdims=True)
    ss_hi = jnp.sum(jnp.where(lo, 0.0, sq), axis=-1, keepdims=True)
    return t * lax.rsqrt(jnp.where(lo, ss_lo, ss_hi) + HEAD_DIM * EPS) * gain2_scaled


def _pair_blockdiag(t):
    lo = _lo_lanes()
    r = pltpu.roll(t, HEAD_DIM, 1)
    return ((jnp.where(lo, t, 0.0), jnp.where(lo, 0.0, r)),
            (jnp.where(lo, r, 0.0), jnp.where(lo, 0.0, t)))


def _rowmax(parts):
    full, out = None, None
    for p in parts:
        w = p.shape[1]
        for c0 in range(0, w - w % LANES, LANES):
            blk = p[:, c0:c0 + LANES]
            full = blk if full is None else jnp.maximum(full, blk)
        if w % LANES:
            r = jnp.max(p[:, w - w % LANES:], axis=-1, keepdims=True)
            out = r if out is None else jnp.maximum(out, r)
    if full is not None:
        r = jnp.max(full, axis=-1, keepdims=True)
        out = r if out is None else jnp.maximum(out, r)
    return out


def _softmax_numerators(sc, sinks):
    s = sc.shape[1] // 2
    b = (s // LANES) * LANES
    left, mid, right = sc[:, :b], sc[:, b:b + LANES], sc[:, b + LANES:]
    in0 = lax.broadcasted_iota(jnp.int32, (1, LANES), 1) < (s - b)
    m0 = _rowmax([left, jnp.where(in0, mid, sinks[0])])
    m1 = _rowmax([jnp.where(in0, sinks[1], mid), right])
    e = [jnp.exp2(left[:, c:c + LANES] - m0) for c in range(0, b, LANES)]
    e.append(jnp.exp2(mid - jnp.where(in0, m0, m1)))
    e += [jnp.exp2(right[:, c:c + LANES] - m1) for c in range(0, right.shape[1], LANES)]
    sink_share = jnp.exp2(sinks[2] - jnp.where(_lo_lanes(), m0, m1))
    return jnp.concatenate(e, axis=1).astype(BF16), sink_share


def _scores(qn, kbd, bias):
    return lax.dot_general(qn, kbd, (((1,), (1,)), ((), ())), preferred_element_type=F32) + bias


def _weighted_values(e, vbd1, sink_share):
    od = jnp.dot(e, vbd1, preferred_element_type=F32)
    return od[:, :LANES] / (od[:, LANES:] + sink_share)


def _band_attn_body(q_ref, k_ref, v_ref, qg_ref, kg_ref, bias_ref, sink_ref, ones_ref, o_ref, pk_ref, pv_ref,
                    kvbd_ref, qn_ref, sc0_ref, sc1_ref, e0_ref, e1_ref, share0_ref, share1_ref, *, seq, qb, fill_rows):
    j = pl.program_id(1)
    pad = BAND_CHUNKS * CHUNK
    span = pad + CHUNK
    rows = fill_rows
    n_pair = GQA_GROUP // 2
    n_chunk = qb // CHUNK
    sc_refs, e_refs, share_refs = (sc0_ref, sc1_ref), (e0_ref, e1_ref), (share0_ref, share1_ref)
    assert n_chunk >= 4 and n_chunk % 2 == 0

    @pl.when(j == 0)
    def _():
        zeros = jnp.zeros((pad, LANES), BF16)
        for a in range(8):
            kvbd_ref[a, 0:pad, :] = zeros

        def fill(i, carry):
            r0 = pl.multiple_of(i * rows, rows)
            kn = _head_norm(k_ref[0, pl.ds(r0, rows), :].astype(F32), kg_ref[...])
            for is_v, t in enumerate((kn, v_ref[0, pl.ds(r0, rows), :].astype(F32))):
                for kv, halves in enumerate(_pair_blockdiag(t)):
                    for tb, half in enumerate(halves):
                        kvbd_ref[4 * is_v + 2 * kv + tb, pl.ds(pad + r0, rows), :] = half.astype(BF16)
            return carry

        lax.fori_loop(0, seq // rows, fill, 0)
        pk_ref[0] = _head_norm(k_ref[0, seq - WINDOW:seq, :].astype(F32), kg_ref[...])
        pv_ref[0] = v_ref[0, seq - WINDOW:seq, :].astype(F32)

    def lanes_of(kv, i):
        return slice((kv * n_pair + i) * LANES, (kv * n_pair + i + 1) * LANES)

    def stage_norm(c):
        q0 = pl.multiple_of(c * CHUNK, CHUNK)
        for kv in range(2):
            for i in range(n_pair):
                qn_ref[c, kv, i * CHUNK:(i + 1) * CHUNK, :] = _head_norm(
                    q_ref[0, pl.ds(q0, CHUNK), lanes_of(kv, i)].astype(F32), qg_ref[...]).astype(BF16)

    def key_rows(c):
        return pl.ds(pl.multiple_of((j * n_chunk + c) * CHUNK, CHUNK), span)

    def stage_scores(c, slot):
        variant = jnp.minimum(j * n_chunk + c, BAND_CHUNKS)
        for kv in range(2):
            kbd = jnp.concatenate([kvbd_ref[2 * kv, key_rows(c), :], kvbd_ref[2 * kv + 1, key_rows(c), :]], axis=0)
            sc_refs[slot][kv] = _scores(qn_ref[c, kv], kbd, bias_ref[variant, kv])

    def stage_exp(slot):
        for kv in range(2):
            e_refs[slot][kv], share_refs[slot][kv] = _softmax_numerators(sc_refs[slot][kv], sink_ref[kv])

    def stage_values(c, slot):
        q0 = pl.multiple_of(c * CHUNK, CHUNK)
        for kv in range(2):
            vbd = jnp.concatenate([kvbd_ref[4 + 2 * kv, key_rows(c), :], kvbd_ref[4 + 2 * kv + 1, key_rows(c), :]], axis=0)
            o = _weighted_values(e_refs[slot][kv], jnp.concatenate([vbd, ones_ref[...]], axis=1), share_refs[slot][kv])
            for i in range(n_pair):
                o_ref[0, pl.ds(q0, CHUNK), lanes_of(kv, i)] = o[i * CHUNK:(i + 1) * CHUNK].astype(BF16)

    def iteration(i, parity):
        live = (lambda c: 0 <= c < n_chunk) if isinstance(i, int) else (lambda c: True)
        if live(i - 3):
            stage_values(i - 3, 1 - parity)
        if live(i - 2):
            stage_exp(parity)
        if live(i - 1):
            stage_scores(i - 1, 1 - parity)
        if live(i):
            stage_norm(i)

    for i in range(4):
        iteration(i, i % 2)

    def steady(p, carry):
        iteration(4 + 2 * p, 0)
        iteration(5 + 2 * p, 1)
        return carry

    lax.fori_loop(0, (n_chunk - 4) // 2, steady, 0)
    for i in range(n_chunk, n_chunk + 3):
        iteration(i, i % 2)


def _band_attn(h3, qcol, kcol, vcol, qgain2, kgain2, bias, sinkcol, ones_bd, qb, fill_rows):
    nb, seq, _ = h3.shape
    aw = 2 * GQA_GROUP * HEAD_DIM
    pad = BAND_CHUNKS * CHUNK
    assert seq % qb == 0 and seq % fill_rows == 0
    body = functools.partial(_band_attn_body, seq=seq, qb=qb, fill_rows=fill_rows)
    const = lambda *shape: pl.BlockSpec(shape, lambda b, j: (0,) * len(shape))
    return pl.pallas_call(
        body,
        grid=(nb, seq // qb),
        in_specs=[pl.BlockSpec((1, qb, aw), lambda b, j: (b, j, qcol // aw)),
                  pl.BlockSpec((1, seq, LANES), lambda b, j: (b, 0, kcol // LANES)),
                  pl.BlockSpec((1, seq, LANES), lambda b, j: (b, 0, vcol // LANES)),
                  const(1, LANES), const(1, LANES),
                  const(*bias.shape), const(*sinkcol.shape), const(*ones_bd.shape)],
        out_specs=[pl.BlockSpec((1, qb, aw), lambda b, j: (b, j, 0)),
                   pl.BlockSpec((1, WINDOW, LANES), lambda b, j: (b, 0, 0)),
                   pl.BlockSpec((1, WINDOW, LANES), lambda b, j: (b, 0, 0))],
        out_shape=[jax.ShapeDtypeStruct((nb, seq, aw), BF16),
                   jax.ShapeDtypeStruct((nb, WINDOW, LANES), F32),
                   jax.ShapeDtypeStruct((nb, WINDOW, LANES), F32)],
        scratch_shapes=[pltpu.VMEM((8, seq + pad, LANES), BF16),
                        pltpu.VMEM((qb // CHUNK, 2, (GQA_GROUP // 2) * CHUNK, LANES), BF16),
                        *[pltpu.VMEM((2, (GQA_GROUP // 2) * CHUNK, 2 * (pad + CHUNK)), F32)] * 2,
                        *[pltpu.VMEM((2, (GQA_GROUP // 2) * CHUNK, 2 * (pad + CHUNK)), BF16)] * 2,
                        *[pltpu.VMEM((2, (GQA_GROUP // 2) * CHUNK, LANES), F32)] * 2],
        compiler_params=pltpu.CompilerParams(
            dimension_semantics=("parallel", "arbitrary"), vmem_limit_bytes=VMEM_LIMIT),
        name="band_attn",
    )(h3, h3, h3, qgain2, kgain2, bias, sinkcol, ones_bd)


def _cached_attn_body(q_ref, k_ref, v_ref, ck_ref, cv_ref, qg_ref, kg_ref, bias_ref, sink_ref, ones_ref,
                      o_ref, sk_ref, sv_ref):
    t = q_ref.shape[1]
    n_pair = GQA_GROUP // 2
    kn = _head_norm(k_ref[0].astype(F32), kg_ref[...])
    vn = v_ref[0].astype(F32)
    sk_ref[0] = kn
    sv_ref[0] = vn
    kbds = _pair_blockdiag(jnp.concatenate([ck_ref[0], kn], axis=0))
    vbds = _pair_blockdiag(jnp.concatenate([cv_ref[0], vn], axis=0))
    for kv in range(2):
        qn = jnp.concatenate(
            [_head_norm(q_ref[0, :, (kv * n_pair + i) * LANES:(kv * n_pair + i + 1) * LANES].astype(F32), qg_ref[...])
             for i in range(n_pair)], axis=0).astype(BF16)
        kbd = jnp.concatenate(kbds[kv], axis=0).astype(BF16)
        vbd1 = jnp.concatenate([jnp.concatenate(vbds[kv], axis=0).astype(BF16), ones_ref[...]], axis=1)
        e, sink_share = _softmax_numerators(_scores(qn, kbd, bias_ref[0, kv]), sink_ref[kv])
        o = _weighted_values(e, vbd1, sink_share)
        for i in range(n_pair):
            o_ref[0, :, (kv * n_pair + i) * LANES:(kv * n_pair + i + 1) * LANES] = o[i * t:(i + 1) * t].astype(BF16)


def _cached_attn(h3, qcol, kcol, vcol, cache_k, cache_v, qgain2, kgain2, bias, sinkcol, ones_bd):
    nb, t, _ = h3.shape
    aw = 2 * GQA_GROUP * HEAD_DIM
    r = cache_k.shape[1]
    const = lambda *shape: pl.BlockSpec(shape, lambda b: (0,) * len(shape))
    return pl.pallas_call(
        _cached_attn_body,
        grid=(nb,),
        in_specs=[pl.BlockSpec((1, t, aw), lambda b: (b, 0, qcol // aw)),
                  pl.BlockSpec((1, t, LANES), lambda b: (b, 0, kcol // LANES)),
                  pl.BlockSpec((1, t, LANES), lambda b: (b, 0, vcol // LANES)),
                  pl.BlockSpec((1, r, LANES), lambda b: (b, 0, 0)),
                  pl.BlockSpec((1, r, LANES), lambda b: (b, 0, 0)),
                  const(1, LANES), const(1, LANES),
                  const(*bias.shape), const(*sinkcol.shape), const(*ones_bd.shape)],
        out_specs=[pl.BlockSpec((1, t, aw), lambda b: (b, 0, 0)),
                   pl.BlockSpec((1, t, LANES), lambda b: (b, 0, 0)),
                   pl.BlockSpec((1, t, LANES), lambda b: (b, 0, 0))],
        out_shape=[jax.ShapeDtypeStruct((nb, t, aw), BF16),
                   jax.ShapeDtypeStruct((nb, t, LANES), F32),
                   jax.ShapeDtypeStruct((nb, t, LANES), F32)],
        compiler_params=pltpu.CompilerParams(
            dimension_semantics=("parallel",), vmem_limit_bytes=VMEM_LIMIT),
        name="cached_attn",
    )(h3, h3, h3, cache_k, cache_v, qgain2, kgain2, bias, sinkcol, ones_bd)


def _sigmoid(x):
    return 1.0 / (1.0 + jnp.exp2(x * -LOG2E))


def _merge_body(x_ref, y_ref, za_ref, o_ref, zb_ref, ga_ref, gb_ref,
                wglu_ref, bglu_ref, woa_ref, wob_ref, wo_ref, out_ref, *, n_part):
    part = x_ref.shape[0] // n_part
    for p in range(n_part):
        r = slice(p * part, (p + 1) * part)
        y = y_ref[r, :].astype(F32)
        g = 0.5 * y * (1.0 + lax.erf(y * (2.0 ** -0.5)))
        gl = jnp.dot(g.astype(BF16), wglu_ref[...], preferred_element_type=F32) + bglu_ref[...]
        za = za_ref[r, :].astype(F32)
        br_a = g * _sigmoid(gl) * (za * _sigmoid(za))
        zb = zb_ref[r, :].astype(F32)
        br_b = o_ref[r, :].astype(F32) * (zb * _sigmoid(zb))
        mixed = (_sigmoid(ga_ref[r, :].astype(F32)) * jnp.dot(br_a.astype(BF16), woa_ref[...], preferred_element_type=F32)
                 + _sigmoid(gb_ref[r, :].astype(F32)) * jnp.dot(br_b.astype(BF16), wob_ref[...], preferred_element_type=F32))
        out_ref[r, :] = x_ref[r, :] + jnp.dot(mixed.astype(BF16), wo_ref[...], preferred_element_type=F32)


def _merge(x2, y2, o2, h2, cols, wglu, bglu, woa, wob, wo, tm, n_part):
    n, d = x2.shape
    sw = y2.shape[1]
    aw = o2.shape[1]
    za_col, zb_col, ga_col, gb_col = cols
    row = lambda width, col: pl.BlockSpec((pl.Element(tm), pl.Element(width)), lambda i: (i * tm, col))
    weight = lambda a: pl.BlockSpec(a.shape, lambda i: (0, 0), pipeline_mode=pl.Buffered(1))
    return pl.pallas_call(
        functools.partial(_merge_body, n_part=n_part),
        grid=(n // tm,),
        in_specs=[row(d, 0), row(sw, 0), row(sw, za_col), row(aw, 0), row(aw, zb_col),
                  row(d, ga_col), row(d, gb_col),
                  weight(wglu), weight(bglu), weight(woa), weight(wob), weight(wo)],
        out_specs=row(d, 0),
        out_shape=jax.ShapeDtypeStruct((n, d), F32),
        compiler_params=pltpu.CompilerParams(
            dimension_semantics=("parallel",), vmem_limit_bytes=VMEM_LIMIT),
        name="merge",
    )(x2, y2, h2, o2, h2, h2, h2, wglu, bglu, woa, wob, wo)


def _discretize(a_re, a_im, log_dt, b_re, b_im):
    dt = jnp.exp(log_dt)[:, None]
    mag = jnp.exp(a_re * dt)
    ang = a_im * dt
    lam_re = mag * jnp.cos(ang)
    lam_im = mag * jnp.sin(ang)
    den = a_re * a_re + a_im * a_im
    cr = ((lam_re - 1.0) * a_re + lam_im * a_im) / den
    ci = (lam_im * a_re - (lam_re - 1.0) * a_im) / den
    bb_re = cr[..., None] * b_re - ci[..., None] * b_im
    bb_im = cr[..., None] * b_im + ci[..., None] * b_re
    return lam_re, lam_im, bb_re, bb_im


def _slab_blockdiag(t):
    n_slab, g, a, b = t.shape
    same_group = jnp.eye(g, dtype=bool)[None, :, None, :, None]
    return jnp.where(same_group, t[:, :, :, None, :], 0.0).reshape(n_slab, g * a, g * b)


def _t5_bucket(rel):
    half = N_BUCKETS // 2
    n = -rel
    ret = jnp.where(n < 0, half, 0)
    n = jnp.abs(n)
    max_exact = half // 2
    nf = jnp.maximum(n, 1).astype(F32)
    large = max_exact + (jnp.log(nf / max_exact) / math.log(MAX_DISTANCE / max_exact)
                         * (half - max_exact)).astype(jnp.int32)
    large = jnp.minimum(large, half - 1)
    return ret + jnp.where(n < max_exact, n, large)


def _pair_bias(rel, table):
    t, s = rel.shape
    onehot = (_t5_bucket(rel)[..., None] == jnp.arange(N_BUCKETS)).astype(F32)
    b = jnp.einsum("tsn,nh->tsh", onehot, table.astype(F32) * LOG2E, precision=lax.Precision.HIGHEST)
    b = b.reshape(t, s, 2, GQA_GROUP // 2, 2)
    return jnp.transpose(b, (2, 3, 0, 4, 1)).reshape(2, (GQA_GROUP // 2) * t, 2 * s)


def _pair_sinks(sinks, t):
    sk = jnp.transpose(sinks.astype(F32).reshape(2, GQA_GROUP // 2, 2), (0, 2, 1))[:, :, :, None, None]
    sk = jnp.broadcast_to(sk, (2, 2, GQA_GROUP // 2, t, LANES)).reshape(2, 2, (GQA_GROUP // 2) * t, LANES)
    lo = jnp.arange(LANES) < HEAD_DIM
    return jnp.concatenate([sk, jnp.where(lo, sk[:, 0], sk[:, 1])[:, None]], axis=1) * LOG2E


def _pair_ones(s):
    first = (jnp.arange(2 * s) < s)[:, None]
    lo = (jnp.arange(LANES) < HEAD_DIM)[None, :]
    return (first == lo).astype(BF16)


class _Tiles(NamedTuple):
    inproj_rows: int
    cast_rows: int
    s5_steps: int
    s5_sub: int
    attn_rows: int
    kv_fill_rows: int
    merge_rows: int
    merge_parts: int


def _prompt_tiles(nb, seq):
    return _Tiles(inproj_rows=min(256, nb * seq), cast_rows=64, s5_steps=min(CHUNK, seq), s5_sub=2,
                  attn_rows=min(2048, seq), kv_fill_rows=min(512, seq), merge_rows=min(512, nb * seq), merge_parts=2)


def kernel(x_prompt, x_sample, cache_k, cache_v, state_ssm_re, state_ssm_im, norm_gain, w_in, ssm_a_re, ssm_a_im, ssm_log_dt, ssm_b_re, ssm_b_im, ssm_c_re, ssm_c_im, ssm_d, w_glu, b_glu, q_gain, k_gain, attn_sinks, rel_bias, w_out_a, w_out_b, w_o):
    nb, seq, d = x_prompt.shape
    db, dseq, _ = x_sample.shape
    n_groups, n_st = ssm_a_re.shape[1:]
    sw = n_groups * SSM_GROUP
    aw = 2 * GQA_GROUP * HEAD_DIM
    kvw = 2 * HEAD_DIM
    n_slab = sw // LANES
    n_state = n_groups * n_st
    l = 0

    c_u, c_za, c_q = 0, sw, 2 * sw
    c_k = c_q + aw
    c_v = c_k + kvw
    c_zb = c_v + kvw
    c_ga = c_zb + aw
    c_gb = c_ga + d
    gain = norm_gain[l].astype(F32).reshape(1, d)

    lam_re, lam_im, bb_re, bb_im = _discretize(
        ssm_a_re[l].astype(F32), ssm_a_im[l].astype(F32), ssm_log_dt[l].astype(F32),
        ssm_b_re[l].astype(F32), ssm_b_im[l].astype(F32))
    lam_b = jnp.broadcast_to(jnp.stack([lam_re.reshape(-1), lam_im.reshape(-1)])[:, None, :], (2, nb, n_state))
    slab = lambda t: t.reshape(n_slab, SLAB_GROUPS, *t.shape[1:])
    bmat = jnp.concatenate([_slab_blockdiag(jnp.swapaxes(slab(bb_re), 2, 3)),
                            _slab_blockdiag(jnp.swapaxes(slab(bb_im), 2, 3))], axis=2).astype(BF16)
    cmat = jnp.concatenate([_slab_blockdiag(jnp.swapaxes(slab(ssm_c_re[l].astype(F32)), 2, 3)),
                            _slab_blockdiag(jnp.swapaxes(slab(-ssm_c_im[l].astype(F32)), 2, 3))], axis=1).astype(BF16)
    d_slab = ssm_d[l].astype(F32).reshape(n_slab, 1, LANES)

    qgain2 = jnp.tile(q_gain[l].astype(F32) * (HEAD_DIM ** 0.5 * SCALE * LOG2E), 2).reshape(1, LANES)
    kgain2 = jnp.tile(k_gain[l].astype(F32) * HEAD_DIM ** 0.5, 2).reshape(1, LANES)
    span = (BAND_CHUNKS + 1) * CHUNK
    rel_p = jnp.arange(span)[None, :] - BAND_CHUNKS * CHUNK - jnp.arange(CHUNK)[:, None]
    bias_p = _pair_bias(rel_p, rel_bias)
    key_off = jnp.tile(jnp.arange(span), 2)[None, None, None, :]
    first_valid = ((BAND_CHUNKS - jnp.arange(BAND_CHUNKS + 1)) * CHUNK)[:, None, None, None]
    bias_p = jnp.where(key_off >= first_valid, bias_p[None], -jnp.inf)
    sink_p = _pair_sinks(attn_sinks[l], CHUNK)
    rows = cache_k.shape[2]
    rel_s = jnp.arange(rows + dseq)[None, :] - rows - jnp.arange(dseq)[:, None]
    bias_s = _pair_bias(rel_s, rel_bias)[None]
    sink_s = _pair_sinks(attn_sinks[l], dseq)

    wglu = w_glu[l].astype(BF16)
    bglu = b_glu[l].astype(F32).reshape(1, sw)
    woa = w_out_a[l].astype(BF16)
    wob = w_out_b[l].astype(BF16)
    wo = w_o[l].astype(BF16)
    merge_cols = (c_za, c_zb, c_ga, c_gb)

    xp2 = x_prompt.reshape(nb * seq, d)
    xs2 = x_sample.reshape(db * dseq, d)
    tiles = _prompt_tiles(nb, seq)
    hp, hs = _inproj(xp2, xs2, gain, w_in[l], tm=tiles.inproj_rows, cast_rows=tiles.cast_rows)
    hp3 = hp.reshape(nb, seq, -1)
    yp, hfin_p = _s5(hp3, jnp.zeros((nb, 2 * n_state), F32), lam_b, bmat, cmat, d_slab,
                     tl=tiles.s5_steps, n_sub=tiles.s5_sub)
    op, pk, pv = _band_attn(hp3, c_q, c_k, c_v, qgain2, kgain2, bias_p, sink_p, _pair_ones(span),
                            qb=tiles.attn_rows, fill_rows=tiles.kv_fill_rows)
    y_p = _merge(xp2, yp.reshape(nb * seq, sw), op.reshape(nb * seq, aw), hp, merge_cols,
                 wglu, bglu, woa, wob, wo, tm=tiles.merge_rows, n_part=tiles.merge_parts).reshape(nb, seq, d)

    hs3 = hs.reshape(db, dseq, -1)
    h0 = jnp.concatenate([state_ssm_re[l].reshape(db, n_state), state_ssm_im[l].reshape(db, n_state)],
                         axis=1).astype(F32)
    ys, hfin_s = _s5(hs3, h0, lam_b, bmat, cmat, d_slab, tl=dseq, n_sub=1)
    ck = cache_k[l].reshape(db, rows, kvw).astype(F32)
    cv = cache_v[l].reshape(db, rows, kvw).astype(F32)
    os_, sk, sv = _cached_attn(hs3, c_q, c_k, c_v, ck, cv, qgain2, kgain2, bias_s, sink_s, _pair_ones(rows + dseq))
    y_s = _merge(xs2, ys.reshape(db * dseq, sw), os_.reshape(db * dseq, aw), hs, merge_cols,
                 wglu, bglu, woa, wob, wo, tm=db * dseq, n_part=1).reshape(db, dseq, d)

    st = lambda h, nbb: (h[:, :n_state].reshape(1, nbb, n_groups, n_st), h[:, n_state:].reshape(1, nbb, n_groups, n_st))
    p_re, p_im = st(hfin_p, nb)
    s_re, s_im = st(hfin_s, db)
    kvshape = lambda a: a.reshape(1, a.shape[0], a.shape[1], 2, HEAD_DIM)
    return (y_p, y_s, p_re, p_im, kvshape(pk), kvshape(pv), s_re, s_im, kvshape(sk), kvshape(sv))
```

```python
import functools
import math
from typing import NamedTuple

import jax
import jax.numpy as jnp
from jax import lax
from jax.experimental import pallas as pl
from jax.experimental.pallas import tpu as pltpu

F32 = jnp.float32
BF16 = jnp.bfloat16

LANES = 128
SUBLANES = 8
VMEM_LIMIT = 56 * 1024 * 1024

CHUNK = 64
HEAD_DIM = 64
GQA_GROUP = 8
SSM_GROUP = 16
SSM_STATE = 64
WINDOW = 128
BAND_CHUNKS = 2
N_BUCKETS = 32
MAX_DISTANCE = 128
EPS = 1e-6
SCALE = HEAD_DIM ** -0.5
LOG2E = math.log2(math.e)
SLAB_GROUPS = LANES // SSM_GROUP
SLAB_STATES = SLAB_GROUPS * SSM_STATE
SCAN_LANES = 1024


def _inproj_body(xp_ref, xs_ref, g_ref, w_hbm, hp_ref, hs_ref, w_ref, stage_ref, sem, *, n_tile, cast_rows):
    i = pl.program_id(0)
    d = w_ref.shape[0]
    n_chunk = d // cast_rows

    def chunk_copy(c, slot):
        return pltpu.make_async_copy(w_hbm.at[pl.ds(c * cast_rows, cast_rows), :], stage_ref.at[slot], sem.at[slot])

    @pl.when(i == 0)
    def _():
        chunk_copy(0, 0).start()

        def cast(c, carry):
            slot = c % 2

            @pl.when(c + 1 < n_chunk)
            def _():
                chunk_copy(c + 1, 1 - slot).start()

            chunk_copy(c, slot).wait()
            w_ref[pl.ds(pl.multiple_of(c * cast_rows, cast_rows), cast_rows), :] = stage_ref[slot].astype(BF16)
            return carry

        lax.fori_loop(0, n_chunk, cast, 0)

    def project(x_ref, o_ref):
        x = x_ref[...]
        r = lax.rsqrt(jnp.mean(x * x, axis=-1, keepdims=True) + EPS)
        xg = (x * g_ref[...]).astype(BF16)
        o_ref[...] = (jnp.dot(xg, w_ref[...], preferred_element_type=F32) * r).astype(BF16)

    @pl.when(i < n_tile)
    def _():
        project(xp_ref, hp_ref)

    @pl.when(i == n_tile)
    def _():
        project(xs_ref, hs_ref)


def _inproj(xp2, xs2, gain, w_f32, tm, cast_rows):
    n, d = xp2.shape
    ns = xs2.shape[0]
    width = w_f32.shape[1]
    n_tile = n // tm
    last = lambda i: (jnp.minimum(i, n_tile - 1), 0)
    return pl.pallas_call(
        functools.partial(_inproj_body, n_tile=n_tile, cast_rows=cast_rows),
        grid=(n_tile + 1,),
        in_specs=[pl.BlockSpec((tm, d), last),
                  pl.BlockSpec((ns, d), lambda i: (0, 0)),
                  pl.BlockSpec((1, d), lambda i: (0, 0)),
                  pl.BlockSpec(memory_space=pl.ANY)],
        out_specs=[pl.BlockSpec((tm, width), last),
                   pl.BlockSpec((ns, width), lambda i: (0, 0))],
        out_shape=[jax.ShapeDtypeStruct((n, width), BF16),
                   jax.ShapeDtypeStruct((ns, width), BF16)],
        scratch_shapes=[pltpu.VMEM((d, width), BF16),
                        pltpu.VMEM((2, cast_rows, width), F32),
                        pltpu.SemaphoreType.DMA((2,))],
        compiler_params=pltpu.CompilerParams(
            dimension_semantics=("arbitrary",), vmem_limit_bytes=VMEM_LIMIT),
        name="inproj",
    )(xp2, xs2, gain, w_f32)


def _s5_body(u_ref, h0_ref, lam_ref, bmat_ref, cmat_ref, d_ref, y_ref, hout_ref,
             utb_ref, ytb_ref, bu_ref, hst_ref, *, nb, tl, n_slab, n_state, n_sub):
    sub_t = tl // n_sub
    sub_m = sub_t * nb
    n_pass = n_state // SCAN_LANES

    @pl.when(pl.program_id(0) == 0)
    def _():
        hst_ref[...] = h0_ref[...]

    for s in range(n_slab):
        for b in range(nb):
            utb_ref[s, pl.ds(b, tl, stride=nb), :] = u_ref[b, :, s * LANES:(s + 1) * LANES].astype(F32)

    def re_cols(s, width=SLAB_STATES):
        return slice(s * width, (s + 1) * width)

    def im_cols(s, width=SLAB_STATES):
        return slice(n_state + s * width, n_state + (s + 1) * width)

    def project_in(k):
        rows = slice(k * sub_m, (k + 1) * sub_m)
        for s in range(n_slab):
            bu = jnp.dot(utb_ref[s, rows, :].astype(BF16), bmat_ref[s], preferred_element_type=F32)
            bu_ref[rows, re_cols(s)] = bu[:, :SLAB_STATES]
            bu_ref[rows, im_cols(s)] = bu[:, SLAB_STATES:]

    def scan(k, state):
        out = []
        for c, (hr, hi) in enumerate(state):
            re, im = re_cols(c, SCAN_LANES), im_cols(c, SCAN_LANES)
            lr = lam_ref[0, :, re]
            li = lam_ref[1, :, re]
            for t in range(k * sub_t, (k + 1) * sub_t):
                r = slice(t * nb, (t + 1) * nb)
                hr, hi = lr * hr - li * hi + bu_ref[r, re], lr * hi + li * hr + bu_ref[r, im]
                bu_ref[r, re] = hr
                bu_ref[r, im] = hi
            out.append((hr, hi))
        return out

    def project_out(k):
        rows = slice(k * sub_m, (k + 1) * sub_m)
        for s in range(n_slab):
            y = (jnp.dot(bu_ref[rows, re_cols(s)].astype(BF16), cmat_ref[s, :SLAB_STATES, :], preferred_element_type=F32)
                 + jnp.dot(bu_ref[rows, im_cols(s)].astype(BF16), cmat_ref[s, SLAB_STATES:, :], preferred_element_type=F32))
            ytb_ref[s, rows, :] = y + d_ref[s] * utb_ref[s, rows, :]

    state = [(hst_ref[:, re_cols(c, SCAN_LANES)], hst_ref[:, im_cols(c, SCAN_LANES)]) for c in range(n_pass)]
    project_in(0)
    for k in range(n_sub):
        if k + 1 < n_sub:
            project_in(k + 1)
        state = scan(k, state)
        if k >= 1:
            project_out(k - 1)
    project_out(n_sub - 1)
    for c, (hr, hi) in enumerate(state):
        hst_ref[:, re_cols(c, SCAN_LANES)] = hr
        hst_ref[:, im_cols(c, SCAN_LANES)] = hi
    hout_ref[...] = hst_ref[...]

    for s in range(n_slab):
        for b in range(nb):
            y_ref[b, :, s * LANES:(s + 1) * LANES] = ytb_ref[s, pl.ds(b, tl, stride=nb), :].astype(BF16)


def _s5(h3, h0, lam_b, bmat, cmat, d_slab, tl, n_sub):
    nb, seq, _ = h3.shape
    n_slab = bmat.shape[0]
    width = n_slab * LANES
    n_state = n_slab * SLAB_STATES
    m = tl * nb
    body = functools.partial(_s5_body, nb=nb, tl=tl, n_slab=n_slab, n_state=n_state, n_sub=n_sub)
    const = lambda *shape: pl.BlockSpec(shape, lambda i: (0,) * len(shape))
    return pl.pallas_call(
        body,
        grid=(seq // tl,),
        in_specs=[pl.BlockSpec((nb, tl, width), lambda i: (0, i, 0)),
                  const(nb, 2 * n_state),
                  const(2, nb, n_state),
                  const(n_slab, LANES, 2 * SLAB_STATES),
                  const(n_slab, 2 * SLAB_STATES, LANES),
                  const(n_slab, 1, LANES)],
        out_specs=[pl.BlockSpec((nb, tl, width), lambda i: (0, i, 0)),
                   const(nb, 2 * n_state)],
        out_shape=[jax.ShapeDtypeStruct((nb, seq, width), BF16),
                   jax.ShapeDtypeStruct((nb, 2 * n_state), F32)],
        scratch_shapes=[pltpu.VMEM((n_slab, m, LANES), F32),
                        pltpu.VMEM((n_slab, m, LANES), F32),
                        pltpu.VMEM((m, 2 * n_state), F32),
                        pltpu.VMEM((nb, 2 * n_state), F32)],
        compiler_params=pltpu.CompilerParams(
            dimension_semantics=("arbitrary",), vmem_limit_bytes=VMEM_LIMIT),
        name="s5",
    )(h3, h0, lam_b, bmat, cmat, d_slab)


def _lo_lanes():
    return lax.broadcasted_iota(jnp.int32, (1, LANES), 1) < HEAD_DIM


def _head_norm(t, gain2_scaled):
    lo = _lo_lanes()
    sq = t * t
    ss_lo = jnp.sum(jnp.where(lo, sq, 0.0), axis=-1, keepdims=True)
    ss_hi = jnp.sum(jnp.where(lo, 0.0, sq), axis=-1, keepdims=True)
    return t * lax.rsqrt(jnp.where(lo, ss_lo, ss_hi) + HEAD_DIM * EPS) * gain2_scaled


def _pair_blockdiag(t):
    lo = _lo_lanes()
    r = pltpu.roll(t, HEAD_DIM, 1)
    return ((jnp.where(lo, t, 0.0), jnp.where(lo, 0.0, r)),
            (jnp.where(lo, r, 0.0), jnp.where(lo, 0.0, t)))


def _rowmax(parts):
    full, out = None, None
    for p in parts:
        w = p.shape[1]
        for c0 in range(0, w - w % LANES, LANES):
            blk = p[:, c0:c0 + LANES]
            full = blk if full is None else jnp.maximum(full, blk)
        if w % LANES:
            r = jnp.max(p[:, w - w % LANES:], axis=-1, keepdims=True)
            out = r if out is None else jnp.maximum(out, r)
    if full is not None:
        r = jnp.max(full, axis=-1, keepdims=True)
        out = r if out is None else jnp.maximum(out, r)
    return out


def _softmax_numerators(sc, sinks):
    s = sc.shape[1] // 2
    b = (s // LANES) * LANES
    left, mid, right = sc[:, :b], sc[:, b:b + LANES], sc[:, b + LANES:]
    in0 = lax.broadcasted_iota(jnp.int32, (1, LANES), 1) < (s - b)
    m0 = _rowmax([left, jnp.where(in0, mid, sinks[0])])
    m1 = _rowmax([jnp.where(in0, sinks[1], mid), right])
    e = [jnp.exp2(left[:, c:c + LANES] - m0) for c in range(0, b, LANES)]
    e.append(jnp.exp2(mid - jnp.where(in0, m0, m1)))
    e += [jnp.exp2(right[:, c:c + LANES] - m1) for c in range(0, right.shape[1], LANES)]
    sink_share = jnp.exp2(sinks[2] - jnp.where(_lo_lanes(), m0, m1))
    return jnp.concatenate(e, axis=1).astype(BF16), sink_share


def _scores(qn, kbd, bias):
    return lax.dot_general(qn, kbd, (((1,), (1,)), ((), ())), preferred_element_type=F32) + bias


def _weighted_values(e, vbd1, sink_share):
    od = jnp.dot(e, vbd1, preferred_element_type=F32)
    return od[:, :LANES] / (od[:, LANES:] + sink_share)


def _band_attn_body(q_ref, k_ref, v_ref, qg_ref, kg_ref, bias_ref, sink_ref, ones_ref, o_ref, pk_ref, pv_ref,
                    kvbd_ref, qn_ref, sc0_ref, sc1_ref, e0_ref, e1_ref, share0_ref, share1_ref, *, seq, qb, fill_rows):
    j = pl.program_id(1)
    pad = BAND_CHUNKS * CHUNK
    span = pad + CHUNK
    rows = fill_rows
    n_pair = GQA_GROUP // 2
    n_chunk = qb // CHUNK
    sc_refs, e_refs, share_refs = (sc0_ref, sc1_ref), (e0_ref, e1_ref), (share0_ref, share1_ref)
    assert n_chunk >= 4 and n_chunk % 2 == 0

    @pl.when(j == 0)
    def _():
        zeros = jnp.zeros((pad, LANES), BF16)
        for a in range(8):
            kvbd_ref[a, 0:pad, :] = zeros

        def fill(i, carry):
            r0 = pl.multiple_of(i * rows, rows)
            kn = _head_norm(k_ref[0, pl.ds(r0, rows), :].astype(F32), kg_ref[...])
            for is_v, t in enumerate((kn, v_ref[0, pl.ds(r0, rows), :].astype(F32))):
                for kv, halves in enumerate(_pair_blockdiag(t)):
                    for tb, half in enumerate(halves):
                        kvbd_ref[4 * is_v + 2 * kv + tb, pl.ds(pad + r0, rows), :] = half.astype(BF16)
            return carry

        lax.fori_loop(0, seq // rows, fill, 0)
        pk_ref[0] = _head_norm(k_ref[0, seq - WINDOW:seq, :].astype(F32), kg_ref[...])
        pv_ref[0] = v_ref[0, seq - WINDOW:seq, :].astype(F32)

    def lanes_of(kv, i):
        return slice((kv * n_pair + i) * LANES, (kv * n_pair + i + 1) * LANES)

    def stage_norm(c):
        q0 = pl.multiple_of(c * CHUNK, CHUNK)
        for kv in range(2):
            for i in range(n_pair):
                qn_ref[c, kv, i * CHUNK:(i + 1) * CHUNK, :] = _head_norm(
                    q_ref[0, pl.ds(q0, CHUNK), lanes_of(kv, i)].astype(F32), qg_ref[...]).astype(BF16)

    def key_rows(c):
        return pl.ds(pl.multiple_of((j * n_chunk + c) * CHUNK, CHUNK), span)

    def stage_scores(c, slot):
        variant = jnp.minimum(j * n_chunk + c, BAND_CHUNKS)
        for kv in range(2):
            kbd = jnp.concatenate([kvbd_ref[2 * kv, key_rows(c), :], kvbd_ref[2 * kv + 1, key_rows(c), :]], axis=0)
            sc_refs[slot][kv] = _scores(qn_ref[c, kv], kbd, bias_ref[variant, kv])

    def stage_exp(slot):
        for kv in range(2):
            e_refs[slot][kv], share_refs[slot][kv] = _softmax_numerators(sc_refs[slot][kv], sink_ref[kv])

    def stage_values(c, slot):
        q0 = pl.multiple_of(c * CHUNK, CHUNK)
        for kv in range(2):
            vbd = jnp.concatenate([kvbd_ref[4 + 2 * kv, key_rows(c), :], kvbd_ref[4 + 2 * kv + 1, key_rows(c), :]], axis=0)
            o = _weighted_values(e_refs[slot][kv], jnp.concatenate([vbd, ones_ref[...]], axis=1), share_refs[slot][kv])
            for i in range(n_pair):
                o_ref[0, pl.ds(q0, CHUNK), lanes_of(kv, i)] = o[i * CHUNK:(i + 1) * CHUNK].astype(BF16)

    def iteration(i, parity):
        live = (lambda c: 0 <= c < n_chunk) if isinstance(i, int) else (lambda c: True)
        if live(i - 3):
            stage_values(i - 3, 1 - parity)
        if live(i - 2):
            stage_exp(parity)
        if live(i - 1):
            stage_scores(i - 1, 1 - parity)
        if live(i):
            stage_norm(i)

    for i in range(4):
        iteration(i, i % 2)

    def steady(p, carry):
        iteration(4 + 2 * p, 0)
        iteration(5 + 2 * p, 1)
        return carry

    lax.fori_loop(0, (n_chunk - 4) // 2, steady, 0)
    for i in range(n_chunk, n_chunk + 3):
        iteration(i, i % 2)


def _band_attn(h3, qcol, kcol, vcol, qgain2, kgain2, bias, sinkcol, ones_bd, qb, fill_rows):
    nb, seq, _ = h3.shape
    aw = 2 * GQA_GROUP * HEAD_DIM
    pad = BAND_CHUNKS * CHUNK
    assert seq % qb == 0 and seq % fill_rows == 0
    body = functools.partial(_band_attn_body, seq=seq, qb=qb, fill_rows=fill_rows)
    const = lambda *shape: pl.BlockSpec(shape, lambda b, j: (0,) * len(shape))
    return pl.pallas_call(
        body,
        grid=(nb, seq // qb),
        in_specs=[pl.BlockSpec((1, qb, aw), lambda b, j: (b, j, qcol // aw)),
                  pl.BlockSpec((1, seq, LANES), lambda b, j: (b, 0, kcol // LANES)),
                  pl.BlockSpec((1, seq, LANES), lambda b, j: (b, 0, vcol // LANES)),
                  const(1, LANES), const(1, LANES),
                  const(*bias.shape), const(*sinkcol.shape), const(*ones_bd.shape)],
        out_specs=[pl.BlockSpec((1, qb, aw), lambda b, j: (b, j, 0)),
                   pl.BlockSpec((1, WINDOW, LANES), lambda b, j: (b, 0, 0)),
                   pl.BlockSpec((1, WINDOW, LANES), lambda b, j: (b, 0, 0))],
        out_shape=[jax.ShapeDtypeStruct((nb, seq, aw), BF16),
                   jax.ShapeDtypeStruct((nb, WINDOW, LANES), F32),
                   jax.ShapeDtypeStruct((nb, WINDOW, LANES), F32)],
        scratch_shapes=[pltpu.VMEM((8, seq + pad, LANES), BF16),
                        pltpu.VMEM((qb // CHUNK, 2, (GQA_GROUP // 2) * CHUNK, LANES), BF16),
                        *[pltpu.VMEM((2, (GQA_GROUP // 2) * CHUNK, 2 * (pad + CHUNK)), F32)] * 2,
                        *[pltpu.VMEM((2, (GQA_GROUP // 2) * CHUNK, 2 * (pad + CHUNK)), BF16)] * 2,
                        *[pltpu.VMEM((2, (GQA_GROUP // 2) * CHUNK, LANES), F32)] * 2],
        compiler_params=pltpu.CompilerParams(
            dimension_semantics=("parallel", "arbitrary"), vmem_limit_bytes=VMEM_LIMIT),
        name="band_attn",
    )(h3, h3, h3, qgain2, kgain2, bias, sinkcol, ones_bd)


def _cached_attn_body(q_ref, k_ref, v_ref, ck_ref, cv_ref, qg_ref, kg_ref, bias_ref, sink_ref, ones_ref,
                      o_ref, sk_ref, sv_ref):
    t = q_ref.shape[1]
    n_pair = GQA_GROUP // 2
    kn = _head_norm(k_ref[0].astype(F32), kg_ref[...])
    vn = v_ref[0].astype(F32)
    sk_ref[0] = kn
    sv_ref[0] = vn
    kbds = _pair_blockdiag(jnp.concatenate([ck_ref[0], kn], axis=0))
    vbds = _pair_blockdiag(jnp.concatenate([cv_ref[0], vn], axis=0))
    for kv in range(2):
        qn = jnp.concatenate(
            [_head_norm(q_ref[0, :, (kv * n_pair + i) * LANES:(kv * n_pair + i + 1) * LANES].astype(F32), qg_ref[...])
             for i in range(n_pair)], axis=0).astype(BF16)
        kbd = jnp.concatenate(kbds[kv], axis=0).astype(BF16)
        vbd1 = jnp.concatenate([jnp.concatenate(vbds[kv], axis=0).astype(BF16), ones_ref[...]], axis=1)
        e, sink_share = _softmax_numerators(_scores(qn, kbd, bias_ref[0, kv]), sink_ref[kv])
        o = _weighted_values(e, vbd1, sink_share)
        for i in range(n_pair):
            o_ref[0, :, (kv * n_pair + i) * LANES:(kv * n_pair + i + 1) * LANES] = o[i * t:(i + 1) * t].astype(BF16)


def _cached_attn(h3, qcol, kcol, vcol, cache_k, cache_v, qgain2, kgain2, bias, sinkcol, ones_bd):
    nb, t, _ = h3.shape
    aw = 2 * GQA_GROUP * HEAD_DIM
    r = cache_k.shape[1]
    const = lambda *shape: pl.BlockSpec(shape, lambda b: (0,) * len(shape))
    return pl.pallas_call(
        _cached_attn_body,
        grid=(nb,),
        in_specs=[pl.BlockSpec((1, t, aw), lambda b: (b, 0, qcol // aw)),
                  pl.BlockSpec((1, t, LANES), lambda b: (b, 0, kcol // LANES)),
                  pl.BlockSpec((1, t, LANES), lambda b: (b, 0, vcol // LANES)),
                  pl.BlockSpec((1, r, LANES), lambda b: (b, 0, 0)),
                  pl.BlockSpec((1, r, LANES), lambda b: (b, 0, 0)),
                  const(1, LANES), const(1, LANES),
                  const(*bias.shape), const(*sinkcol.shape), const(*ones_bd.shape)],
        out_specs=[pl.BlockSpec((1, t, aw), lambda b: (b, 0, 0)),
                   pl.BlockSpec((1, t, LANES), lambda b: (b, 0, 0)),
                   pl.BlockSpec((1, t, LANES), lambda b: (b, 0, 0))],
        out_shape=[jax.ShapeDtypeStruct((nb, t, aw), BF16),
                   jax.ShapeDtypeStruct((nb, t, LANES), F32),
                   jax.ShapeDtypeStruct((nb, t, LANES), F32)],
        compiler_params=pltpu.CompilerParams(
            dimension_semantics=("parallel",), vmem_limit_bytes=VMEM_LIMIT),
        name="cached_attn",
    )(h3, h3, h3, cache_k, cache_v, qgain2, kgain2, bias, sinkcol, ones_bd)


def _sigmoid(x):
    return 1.0 / (1.0 + jnp.exp2(x * -LOG2E))


def _merge_body(x_ref, y_ref, za_ref, o_ref, zgg_ref,
                wglu_ref, bglu_ref, woa_ref, wob_ref, wo_ref, out_ref, *, n_part):
    part = x_ref.shape[0] // n_part
    aw, d = o_ref.shape[1], x_ref.shape[1]
    zb_ref, ga_ref, gb_ref = (zgg_ref.at[:, 0:aw], zgg_ref.at[:, aw:aw + d], zgg_ref.at[:, aw + d:aw + 2 * d])
    for p in range(n_part):
        r = slice(p * part, (p + 1) * part)
        y = y_ref[r, :].astype(F32)
        g = 0.5 * y * (1.0 + lax.erf(y * (2.0 ** -0.5)))
        gl = jnp.dot(g.astype(BF16), wglu_ref[...], preferred_element_type=F32) + bglu_ref[...]
        za = za_ref[r, :].astype(F32)
        br_a = g * _sigmoid(gl) * (za * _sigmoid(za))
        zb = zb_ref[r, :].astype(F32)
        br_b = o_ref[r, :].astype(F32) * (zb * _sigmoid(zb))
        mixed = (_sigmoid(ga_ref[r, :].astype(F32)) * jnp.dot(br_a.astype(BF16), woa_ref[...], preferred_element_type=F32)
                 + _sigmoid(gb_ref[r, :].astype(F32)) * jnp.dot(br_b.astype(BF16), wob_ref[...], preferred_element_type=F32))
        out_ref[r, :] = x_ref[r, :] + jnp.dot(mixed.astype(BF16), wo_ref[...], preferred_element_type=F32)


def _merge(x2, y2, o2, h2, cols, wglu, bglu, woa, wob, wo, tm, n_part):
    n, d = x2.shape
    sw = y2.shape[1]
    aw = o2.shape[1]
    za_col, zb_col, ga_col, gb_col = cols
    assert ga_col == zb_col + aw and gb_col == ga_col + d
    row = lambda width, col: pl.BlockSpec((pl.Element(tm), pl.Element(width)), lambda i: (i * tm, col))
    weight = lambda a: pl.BlockSpec(a.shape, lambda i: (0, 0), pipeline_mode=pl.Buffered(1))
    return pl.pallas_call(
        functools.partial(_merge_body, n_part=n_part),
        grid=(n // tm,),
        in_specs=[row(d, 0), row(sw, 0), row(sw, za_col), row(aw, 0), row(aw + 2 * d, zb_col),
                  weight(wglu), weight(bglu), weight(woa), weight(wob), weight(wo)],
        out_specs=row(d, 0),
        out_shape=jax.ShapeDtypeStruct((n, d), F32),
        compiler_params=pltpu.CompilerParams(
            dimension_semantics=("parallel",), vmem_limit_bytes=VMEM_LIMIT),
        name="merge",
    )(x2, y2, h2, o2, h2, wglu, bglu, woa, wob, wo)


def _discretize(a_re, a_im, log_dt, b_re, b_im):
    dt = jnp.exp(log_dt)[:, None]
    mag = jnp.exp(a_re * dt)
    ang = a_im * dt
    lam_re = mag * jnp.cos(ang)
    lam_im = mag * jnp.sin(ang)
    den = a_re * a_re + a_im * a_im
    cr = ((lam_re - 1.0) * a_re + lam_im * a_im) / den
    ci = (lam_im * a_re - (lam_re - 1.0) * a_im) / den
    bb_re = cr[..., None] * b_re - ci[..., None] * b_im
    bb_im = cr[..., None] * b_im + ci[..., None] * b_re
    return lam_re, lam_im, bb_re, bb_im


def _slab_blockdiag(t):
    n_slab, g, a, b = t.shape
    same_group = jnp.eye(g, dtype=bool)[None, :, None, :, None]
    return jnp.where(same_group, t[:, :, :, None, :], 0.0).reshape(n_slab, g * a, g * b)


def _t5_bucket(rel):
    half = N_BUCKETS // 2
    n = -rel
    ret = jnp.where(n < 0, half, 0)
    n = jnp.abs(n)
    max_exact = half // 2
    nf = jnp.maximum(n, 1).astype(F32)
    large = max_exact + (jnp.log(nf / max_exact) / math.log(MAX_DISTANCE / max_exact)
                         * (half - max_exact)).astype(jnp.int32)
    large = jnp.minimum(large, half - 1)
    return ret + jnp.where(n < max_exact, n, large)


def _pair_bias(rel, table):
    t, s = rel.shape
    onehot = (_t5_bucket(rel)[..., None] == jnp.arange(N_BUCKETS)).astype(F32)
    b = jnp.einsum("tsn,nh->tsh", onehot, table.astype(F32) * LOG2E, precision=lax.Precision.HIGHEST)
    b = b.reshape(t, s, 2, GQA_GROUP // 2, 2)
    return jnp.transpose(b, (2, 3, 0, 4, 1)).reshape(2, (GQA_GROUP // 2) * t, 2 * s)


def _pair_sinks(sinks, t):
    sk = jnp.transpose(sinks.astype(F32).reshape(2, GQA_GROUP // 2, 2), (0, 2, 1))[:, :, :, None, None]
    sk = jnp.broadcast_to(sk, (2, 2, GQA_GROUP // 2, t, LANES)).reshape(2, 2, (GQA_GROUP // 2) * t, LANES)
    lo = jnp.arange(LANES) < HEAD_DIM
    return jnp.concatenate([sk, jnp.where(lo, sk[:, 0], sk[:, 1])[:, None]], axis=1) * LOG2E


def _pair_ones(s):
    first = (jnp.arange(2 * s) < s)[:, None]
    lo = (jnp.arange(LANES) < HEAD_DIM)[None, :]
    return (first == lo).astype(BF16)


class _Tiles(NamedTuple):
    inproj_rows: int
    cast_rows: int
    s5_steps: int
    s5_sub: int
    attn_rows: int
    kv_fill_rows: int
    merge_rows: int
    merge_parts: int


def _prompt_tiles(nb, seq):
    return _Tiles(inproj_rows=min(256, nb * seq), cast_rows=64, s5_steps=min(CHUNK, seq), s5_sub=2,
                  attn_rows=min(2048, seq), kv_fill_rows=min(512, seq), merge_rows=min(512, nb * seq), merge_parts=2)


def kernel(x_prompt, x_sample, cache_k, cache_v, state_ssm_re, state_ssm_im, norm_gain, w_in, ssm_a_re, ssm_a_im, ssm_log_dt, ssm_b_re, ssm_b_im, ssm_c_re, ssm_c_im, ssm_d, w_glu, b_glu, q_gain, k_gain, attn_sinks, rel_bias, w_out_a, w_out_b, w_o):
    nb, seq, d = x_prompt.shape
    db, dseq, _ = x_sample.shape
    n_groups, n_st = ssm_a_re.shape[1:]
    sw = n_groups * SSM_GROUP
    aw = 2 * GQA_GROUP * HEAD_DIM
    kvw = 2 * HEAD_DIM
    n_slab = sw // LANES
    n_state = n_groups * n_st
    l = 0

    c_u, c_za, c_q = 0, sw, 2 * sw
    c_k = c_q + aw
    c_v = c_k + kvw
    c_zb = c_v + kvw
    c_ga = c_zb + aw
    c_gb = c_ga + d
    gain = norm_gain[l].astype(F32).reshape(1, d)

    lam_re, lam_im, bb_re, bb_im = _discretize(
        ssm_a_re[l].astype(F32), ssm_a_im[l].astype(F32), ssm_log_dt[l].astype(F32),
        ssm_b_re[l].astype(F32), ssm_b_im[l].astype(F32))
    lam_b = jnp.broadcast_to(jnp.stack([lam_re.reshape(-1), lam_im.reshape(-1)])[:, None, :], (2, nb, n_state))
    slab = lambda t: t.reshape(n_slab, SLAB_GROUPS, *t.shape[1:])
    bmat = jnp.concatenate([_slab_blockdiag(jnp.swapaxes(slab(bb_re), 2, 3)),
                            _slab_blockdiag(jnp.swapaxes(slab(bb_im), 2, 3))], axis=2).astype(BF16)
    cmat = jnp.concatenate([_slab_blockdiag(jnp.swapaxes(slab(ssm_c_re[l].astype(F32)), 2, 3)),
                            _slab_blockdiag(jnp.swapaxes(slab(-ssm_c_im[l].astype(F32)), 2, 3))], axis=1).astype(BF16)
    d_slab = ssm_d[l].astype(F32).reshape(n_slab, 1, LANES)

    qgain2 = jnp.tile(q_gain[l].astype(F32) * (HEAD_DIM ** 0.5 * SCALE * LOG2E), 2).reshape(1, LANES)
    kgain2 = jnp.tile(k_gain[l].astype(F32) * HEAD_DIM ** 0.5, 2).reshape(1, LANES)
    span = (BAND_CHUNKS + 1) * CHUNK
    rel_p = jnp.arange(span)[None, :] - BAND_CHUNKS * CHUNK - jnp.arange(CHUNK)[:, None]
    bias_p = _pair_bias(rel_p, rel_bias)
    key_off = jnp.tile(jnp.arange(span), 2)[None, None, None, :]
    first_valid = ((BAND_CHUNKS - jnp.arange(BAND_CHUNKS + 1)) * CHUNK)[:, None, None, None]
    bias_p = jnp.where(key_off >= first_valid, bias_p[None], -jnp.inf)
    sink_p = _pair_sinks(attn_sinks[l], CHUNK)
    rows = cache_k.shape[2]
    rel_s = jnp.arange(rows + dseq)[None, :] - rows - jnp.arange(dseq)[:, None]
    bias_s = _pair_bias(rel_s, rel_bias)[None]
    sink_s = _pair_sinks(attn_sinks[l], dseq)

    wglu = w_glu[l].astype(BF16)
    bglu = b_glu[l].astype(F32).reshape(1, sw)
    woa = w_out_a[l].astype(BF16)
    wob = w_out_b[l].astype(BF16)
    wo = w_o[l].astype(BF16)
    merge_cols = (c_za, c_zb, c_ga, c_gb)

    xp2 = x_prompt.reshape(nb * seq, d)
    xs2 = x_sample.reshape(db * dseq, d)
    tiles = _prompt_tiles(nb, seq)
    hp, hs = _inproj(xp2, xs2, gain, w_in[l], tm=tiles.inproj_rows, cast_rows=tiles.cast_rows)
    hp3 = hp.reshape(nb, seq, -1)
    yp, hfin_p = _s5(hp3, jnp.zeros((nb, 2 * n_state), F32), lam_b, bmat, cmat, d_slab,
                     tl=tiles.s5_steps, n_sub=tiles.s5_sub)
    op, pk, pv = _band_attn(hp3, c_q, c_k, c_v, qgain2, kgain2, bias_p, sink_p, _pair_ones(span),
                            qb=tiles.attn_rows, fill_rows=tiles.kv_fill_rows)
    y_p = _merge(xp2, yp.reshape(nb * seq, sw), op.reshape(nb * seq, aw), hp, merge_cols,
                 wglu, bglu, woa, wob, wo, tm=tiles.merge_rows, n_part=tiles.merge_parts).reshape(nb, seq, d)

    hs3 = hs.reshape(db, dseq, -1)
    h0 = jnp.concatenate([state_ssm_re[l].reshape(db, n_state), state_ssm_im[l].reshape(db, n_state)],
                         axis=1).astype(F32)
    ys, hfin_s = _s5(hs3, h0, lam_b, bmat, cmat, d_slab, tl=dseq, n_sub=1)
    ck = cache_k[l].reshape(db, rows, kvw).astype(F32)
    cv = cache_v[l].reshape(db, rows, kvw).astype(F32)
    os_, sk, sv = _cached_attn(hs3, c_q, c_k, c_v, ck, cv, qgain2, kgain2, bias_s, sink_s, _pair_ones(rows + dseq))
    y_s = _merge(xs2, ys.reshape(db * dseq, sw), os_.reshape(db * dseq, aw), hs, merge_cols,
                 wglu, bglu, woa, wob, wo, tm=db * dseq, n_part=1).reshape(db, dseq, d)

    st = lambda h, nbb: (h[:, :n_state].reshape(1, nbb, n_groups, n_st), h[:, n_state:].reshape(1, nbb, n_groups, n_st))
    p_re, p_im = st(hfin_p, nb)
    s_re, s_im = st(hfin_s, db)
    kvshape = lambda a: a.reshape(1, a.shape[0], a.shape[1], 2, HEAD_DIM)
    return (y_p, y_s, p_re, p_im, kvshape(pk), kvshape(pv), s_re, s_im, kvshape(sk), kvshape(sv))
```
